```python
import math
import jax, jax.numpy as jnp
from jax import lax
import numpy as np


D_MODEL = 1024
BATCH = 32
SEQ = 2048
DEPTH = 2
DEC_BATCH = 16
DEC_SEQ = 4096
PAST_LEN = 128

GRID_W = 64
Q_BLOCK = 128
ROPE_THETA = 10000.0
MLA_HEADS = 6
MLA_Q_RANK = 256
MLA_KV_RANK = 128
MLA_NOPE = 64
MLA_ROPE = 32
MLA_V = 64
GQA_HEADS = 6
GQA_KV_HEADS = 2
GQA_DIM = 64
AXIAL_HALF = GQA_DIM // 2
DIFF_HEADS = 4
DIFF_QK = 32
DIFF_V = 2 * DIFF_QK
MIX_WIDTH = MLA_HEADS * MLA_V + GQA_HEADS * GQA_DIM + DIFF_HEADS * DIFF_V
IN_SPLITS = (MLA_Q_RANK, MLA_KV_RANK, MLA_ROPE,
             GQA_HEADS * GQA_DIM, GQA_KV_HEADS * GQA_DIM, GQA_KV_HEADS * GQA_DIM,
             DIFF_HEADS * 2 * DIFF_QK, DIFF_HEADS * 2 * DIFF_QK, DIFF_HEADS * DIFF_V)
IN_WIDTH = sum(IN_SPLITS)
D_FF = 2816
N_EXPERTS = 8
TOP_K = 2
N_DENSE = (DEPTH + 1) // 2
N_MOE = DEPTH // 2
ALPHA = (2 * DEPTH) ** 0.25
BETA = (8 * DEPTH) ** -0.25
LN_EPS = 1e-5
RMS_EPS = 1e-6

kernel_name = 'hybrid_parallel_mla_gqa_diff_encoder'


def _rmsnorm(x, g):
    xf = x.astype(jnp.float32)
    y = xf * lax.rsqrt(jnp.mean(xf * xf, axis=-1, keepdims=True) + RMS_EPS)
    return y.astype(x.dtype) * g


def _layernorm(x, g, b):
    xf = x.astype(jnp.float32)
    mu = jnp.mean(xf, axis=-1, keepdims=True)
    var = jnp.mean(jnp.square(xf - mu), axis=-1, keepdims=True)
    y = (xf - mu) * lax.rsqrt(var + LN_EPS)
    return y.astype(x.dtype) * g + b


def _rope_cs(pos, dim):
    inv = ROPE_THETA ** (-jnp.arange(0, dim, 2, dtype=jnp.float32) / dim)
    ang = pos.astype(jnp.float32)[:, None] * inv[None, :]
    return jnp.cos(ang), jnp.sin(ang)


def _apply_rope(x, cs):
    cos, sin = cs
    shape = (1, cos.shape[0]) + (1,) * (x.ndim - 3) + (cos.shape[1],)
    c = cos.reshape(shape).astype(x.dtype)
    s = sin.reshape(shape).astype(x.dtype)
    half = x.shape[-1] // 2
    x1, x2 = x[..., :half], x[..., half:]
    return jnp.concatenate([x1 * c - x2 * s, x2 * c + x1 * s], axis=-1)


def _apply_axial(x, row_cs, col_cs):
    return jnp.concatenate([_apply_rope(x[..., :AXIAL_HALF], row_cs),
                            _apply_rope(x[..., AXIAL_HALF:], col_cs)], axis=-1)


def _to_blocks(a):
    b, s = a.shape[:2]
    return jnp.moveaxis(a.reshape((b, s // Q_BLOCK, Q_BLOCK) + a.shape[2:]), 1, 0)


def _from_blocks(a):
    a = jnp.moveaxis(a, 0, 1)
    return a.reshape((a.shape[0], a.shape[1] * a.shape[2]) + a.shape[3:])


def _softmax32(s):
    return jax.nn.softmax(s.astype(jnp.float32), axis=-1)


def _mla_attention(q_nope, q_rope, k_nope, k_rope, v):
    scale = (MLA_NOPE + MLA_ROPE) ** -0.5

    def block(qs):
        qn, qr = qs
        s = (jnp.einsum('bqhd,bkhd->bhqk', qn, k_nope)
             + jnp.einsum('bqhr,bkr->bhqk', qr, k_rope)) * scale
        p = _softmax32(s).astype(v.dtype)
        return jnp.einsum('bhqk,bkhd->bqhd', p, v)

    return _from_blocks(lax.map(block, (_to_blocks(q_nope), _to_blocks(q_rope))))


def _gqa_attention(q, k, v):
    scale = GQA_DIM ** -0.5

    def block(qb):
        s = jnp.einsum('bqngd,bknd->bngqk', qb, k) * scale
        p = _softmax32(s).astype(v.dtype)
        return jnp.einsum('bngqk,bknd->bqngd', p, v)

    return _from_blocks(lax.map(block, _to_blocks(q)))


def _diff_attention(q1, q2, k1, k2, v, lam):
    scale = DIFF_QK ** -0.5

    def block(qs):
        a, b = qs
        p1 = _softmax32(jnp.einsum('bqhd,bkhd->bhqk', a, k1) * scale)
        p2 = _softmax32(jnp.einsum('bqhd,bkhd->bhqk', b, k2) * scale)
        w = (p1 - lam * p2).astype(v.dtype)
        return jnp.einsum('bhqk,bkhd->bqhd', w, v)

    return _from_blocks(lax.map(block, (_to_blocks(q1), _to_blocks(q2))))


def _swiglu(x, wg, wu, wd):
    return (jax.nn.silu(x @ wg) * (x @ wu)) @ wd


def _normal(k, shape, scale):
    return jax.random.normal(k, shape, jnp.float32) * scale


def setup_inputs(seed: int = 0) -> dict:
    key = jax.random.key(seed)
    ks = jax.random.split(key, 24)
    L = DEPTH
    return {
        'x_prompt': _normal(ks[0], (BATCH, SEQ, D_MODEL), 1.0),
        'x_sample': _normal(ks[1], (DEC_BATCH, DEC_SEQ, D_MODEL), 1.0),
        'w_in': _normal(ks[2], (L, D_MODEL, IN_WIDTH), D_MODEL ** -0.5),
        'mla_q_norm': 1.0 + _normal(ks[3], (L, MLA_Q_RANK), 0.02),
        'mla_w_qb': _normal(ks[4], (L, MLA_Q_RANK, MLA_HEADS * (MLA_NOPE + MLA_ROPE)), MLA_Q_RANK ** -0.5),
        'mla_kv_norm': 1.0 + _normal(ks[5], (L, MLA_KV_RANK), 0.02),
        'mla_w_kvb': _normal(ks[6], (L, MLA_KV_RANK, MLA_HEADS * (MLA_NOPE + MLA_V)), MLA_KV_RANK ** -0.5),
        'gqa_q_norm': 1.0 + _normal(ks[7], (L, GQA_DIM), 0.02),
        'gqa_k_norm': 1.0 + _normal(ks[8], (L, GQA_DIM), 0.02),
        'diff_lambda': _normal(ks[9], (L, 4, DIFF_QK), 0.1),
        'diff_out_norm': 1.0 + _normal(ks[10], (L, DIFF_V), 0.02),
        'w_out': _normal(ks[11], (L, MIX_WIDTH, D_MODEL), BETA * MIX_WIDTH ** -0.5),
        'ln1_g': 1.0 + _normal(ks[12], (L, D_MODEL), 0.02),
        'ln1_b': _normal(ks[13], (L, D_MODEL), 0.02),
        'ffn_w_gate': _normal(ks[14], (N_DENSE, D_MODEL, D_FF), D_MODEL ** -0.5),
        'ffn_w_up': _normal(ks[15], (N_DENSE, D_MODEL, D_FF), D_MODEL ** -0.5),
        'ffn_w_down': _normal(ks[16], (N_DENSE, D_FF, D_MODEL), BETA * D_FF ** -0.5),
        'moe_router': _normal(ks[17], (N_MOE, D_MODEL, N_EXPERTS), D_MODEL ** -0.5),
        'moe_w_gate': _normal(ks[18], (N_MOE, N_EXPERTS, D_MODEL, D_FF), D_MODEL ** -0.5),
        'moe_w_up': _normal(ks[19], (N_MOE, N_EXPERTS, D_MODEL, D_FF), D_MODEL ** -0.5),
        'moe_w_down': _normal(ks[20], (N_MOE, N_EXPERTS, D_FF, D_MODEL), BETA * D_FF ** -0.5),
        'ln2_g': 1.0 + _normal(ks[21], (L, D_MODEL), 0.02),
        'ln2_b': _normal(ks[22], (L, D_MODEL), 0.02),
    }


def reference(x_prompt, x_sample, w_in, mla_q_norm, mla_w_qb, mla_kv_norm, mla_w_kvb,
              gqa_q_norm, gqa_k_norm, diff_lambda, diff_out_norm, w_out, ln1_g, ln1_b,
              ffn_w_gate, ffn_w_up, ffn_w_down, moe_router, moe_w_gate, moe_w_up,
              moe_w_down, ln2_g, ln2_b):
    offsets = [int(o) for o in np.cumsum(IN_SPLITS)[:-1]]

    def mixer(x, l, cs_pos, cs_row, cs_col):
        B, S, _ = x.shape
        h = x @ w_in[l]
        cq, ckv, kr, gq, gk, gv, dq, dk, dv = jnp.split(h, offsets, axis=-1)

        q = (_rmsnorm(cq, mla_q_norm[l]) @ mla_w_qb[l]).reshape(B, S, MLA_HEADS, MLA_NOPE + MLA_ROPE)
        q_nope = q[..., :MLA_NOPE]
        q_rope = _apply_rope(q[..., MLA_NOPE:], cs_pos)
        kv = (_rmsnorm(ckv, mla_kv_norm[l]) @ mla_w_kvb[l]).reshape(B, S, MLA_HEADS, MLA_NOPE + MLA_V)
        k_nope, v_mla = kv[..., :MLA_NOPE], kv[..., MLA_NOPE:]
        k_rope = _apply_rope(kr[:, :, None, :], cs_pos)[:, :, 0]
        o_mla = _mla_attention(q_nope, q_rope, k_nope, k_rope, v_mla).reshape(B, S, MLA_HEADS * MLA_V)

        qg = _rmsnorm(gq.reshape(B, S, GQA_HEADS, GQA_DIM), gqa_q_norm[l])
        kg = _rmsnorm(gk.reshape(B, S, GQA_KV_HEADS, GQA_DIM), gqa_k_norm[l])
        vg = gv.reshape(B, S, GQA_KV_HEADS, GQA_DIM)
        qg = _apply_axial(qg, cs_row, cs_col).reshape(B, S, GQA_KV_HEADS, GQA_HEADS // GQA_KV_HEADS, GQA_DIM)
        kg = _apply_axial(kg, cs_row, cs_col)
        o_gqa = _gqa_attention(qg, kg, vg).reshape(B, S, GQA_HEADS * GQA_DIM)

        qd = _apply_rope(dq.reshape(B, S, DIFF_HEADS, 2, DIFF_QK), cs_pos)
        kd = _apply_rope(dk.reshape(B, S, DIFF_HEADS, 2, DIFF_QK), cs_pos)
        vd = dv.reshape(B, S, DIFF_HEADS, DIFF_V)
        lam_init = 0.8 - 0.6 * math.exp(-0.3 * l)
        lp = diff_lambda[l].astype(jnp.float32)
        lam = jnp.exp(jnp.sum(lp[0] * lp[1])) - jnp.exp(jnp.sum(lp[2] * lp[3])) + lam_init
        o_diff = _diff_attention(qd[..., 0, :], qd[..., 1, :], kd[..., 0, :], kd[..., 1, :], vd, lam)
        o_diff = (_rmsnorm(o_diff, diff_out_norm[l]) * (1.0 - lam_init)).reshape(B, S, DIFF_HEADS * DIFF_V)

        return jnp.concatenate([o_mla, o_gqa, o_diff], axis=-1) @ w_out[l]

    def channel(x, l):
        B, S, D = x.shape
        m = l // 2
        if l % 2 == 0:
            return _swiglu(x, ffn_w_gate[m], ffn_w_up[m], ffn_w_down[m])
        xt = x.reshape(B * S, D)
        logits = (xt @ moe_router[m]).astype(jnp.float32)
        top_v, top_i = lax.top_k(logits, TOP_K)
        top_w = jax.nn.softmax(top_v, axis=-1)
        gates = jnp.sum(jax.nn.one_hot(top_i, N_EXPERTS, dtype=jnp.float32) * top_w[..., None],
                        axis=1).astype(x.dtype)
        out = jnp.zeros_like(xt)
        for e in range(N_EXPERTS):
            out = out + gates[:, e:e + 1] * _swiglu(xt, moe_w_gate[m, e], moe_w_up[m, e], moe_w_down[m, e])
        return out.reshape(B, S, D)

    def run(x):
        S = x.shape[1]
        ROWS = S // GRID_W
        pos = jnp.arange(S)
        row = jnp.broadcast_to(jnp.arange(ROWS)[:, None], (ROWS, GRID_W)).reshape(S)
        col = jnp.broadcast_to(jnp.arange(GRID_W)[None, :], (ROWS, GRID_W)).reshape(S)
        cs_pos = _rope_cs(pos, MLA_ROPE)
        cs_row = _rope_cs(row, AXIAL_HALF)
        cs_col = _rope_cs(col, AXIAL_HALF)
        for l in range(DEPTH):
            x = _layernorm(ALPHA * x + mixer(x, l, cs_pos, cs_row, cs_col), ln1_g[l], ln1_b[l])
            x = _layernorm(ALPHA * x + channel(x, l), ln2_g[l], ln2_b[l])
        return x

    y_prompt = run(x_prompt)
    y_sample = run(x_sample)
    return (y_prompt, y_sample)
```

```python
import functools
import math

import jax
import jax.numpy as jnp
from jax import lax
from jax.experimental import pallas as pl
from jax.experimental.pallas import tpu as pltpu

D_MODEL = 1024
DEPTH = 2
GRID_W = 64
ROPE_THETA = 10000.0
MLA_HEADS = 6
MLA_Q_RANK = 256
MLA_KV_RANK = 128
MLA_NOPE = 64
MLA_ROPE = 32
MLA_V = 64
GQA_HEADS = 6
GQA_KV_HEADS = 2
GQA_DIM = 64
DIFF_HEADS = 4
DIFF_QK = 32
DIFF_V = 64
D_FF = 2816
N_EXPERTS = 8
ALPHA = (2 * DEPTH) ** 0.25
LN_EPS = 1e-5
RMS_EPS = 1e-6

MLA_SCALE = (MLA_NOPE + MLA_ROPE) ** -0.5
GQA_SCALE = GQA_DIM ** -0.5
DIFF_SCALE = DIFF_QK ** -0.5

LANES = 128
VMEM_LIMIT = 56 * 1024 * 1024
MAX_SEQ = 4096

PREP_TM = 256
ATTN_TQ = 256
PROJ_TM = 512
FFN_TM = 512
FFN_TF = 1408

BF16 = jnp.bfloat16
F32 = jnp.float32

_SEG = dict(cq=(0, 256), ckv=(256, 384), kr=(384, 512), krr=(512, 640),
            gq=(640, 1024), gqr=(1024, 1408), gk=(1408, 1536), gkr=(1536, 1664),
            gv=(1664, 1792), dq=(1792, 2048), dqr=(2048, 2304), dk=(2304, 2560),
            dkr=(2560, 2816), dv=(2816, 3072))
PREP_N = 3072
GQA_ORDER = (0, 3, 1, 4, 2, 5)


def _rot_cols(w, half=16):
    k, n = w.shape
    w4 = w.reshape(k, n // (2 * half), 2, half)
    return jnp.stack([-w4[:, :, 1, :], w4[:, :, 0, :]], axis=2).reshape(k, n)


def _rope_tables():
    pos = jnp.arange(MAX_SEQ, dtype=F32)
    inv = ROPE_THETA ** (-jnp.arange(0, 32, 2, dtype=F32) / 32)

    def cs(p):
        ang = p[:, None] * inv[None, :]
        c, s = jnp.cos(ang), jnp.sin(ang)
        return jnp.concatenate([c, c], -1), jnp.concatenate([s, s], -1)

    c32, s32 = cs(pos)
    row = jnp.floor(pos / GRID_W)
    col = pos - row * GRID_W
    cr, sr = cs(row)
    cc, sc = cs(col)
    ones = jnp.ones((MAX_SEQ, 64), F32)
    z64 = jnp.zeros((MAX_SEQ, 64), F32)
    z32 = jnp.zeros((MAX_SEQ, 32), F32)
    a_c = jnp.concatenate([ones, c32, z32], -1)
    a_s = jnp.concatenate([z64, s32, z32], -1)
    p_c = jnp.tile(c32, (1, 4))
    p_s = jnp.tile(s32, (1, 4))
    x_c = jnp.tile(jnp.concatenate([cr, cc], -1), (1, 2))
    x_s = jnp.tile(jnp.concatenate([sr, sc], -1), (1, 2))
    return a_c, a_s, p_c, p_s, x_c, x_s


def _layer_weights(l, w_in, mla_q_norm, mla_w_qb, mla_kv_norm, mla_w_kvb, gqa_q_norm,
                   gqa_k_norm, w_out):
    w = w_in[l]
    o = [0, 256, 384, 416, 800, 928, 1056, 1312, 1568, 1824]
    cq, ckv, kr, gq, gk, gv, dq, dk, dv = [w[:, o[i]:o[i + 1]] for i in range(9)]
    z = lambda n: jnp.zeros((D_MODEL, n), F32)
    kr_p = jnp.concatenate([z(64), kr, z(32)], -1)
    krr_p = jnp.concatenate([z(64), _rot_cols(kr), z(32)], -1)
    order = jnp.asarray(GQA_ORDER)
    gq_p = gq.reshape(D_MODEL, GQA_HEADS, GQA_DIM)[:, order, :].reshape(D_MODEL, -1)
    w_all = jnp.concatenate(
        [cq, ckv, kr_p, krr_p, gq_p, _rot_cols(gq_p), gk, _rot_cols(gk), gv,
         dq, _rot_cols(dq), dk, _rot_cols(dk), dv], axis=-1).astype(BF16)

    qb = mla_w_qb[l].reshape(MLA_Q_RANK, MLA_HEADS, MLA_NOPE + MLA_ROPE)
    qb_nope, qb_rope = qb[:, :, :MLA_NOPE], qb[:, :, MLA_NOPE:]
    qb_rope_rot = _rot_cols(qb_rope.reshape(MLA_Q_RANK, -1)).reshape(qb_rope.shape)
    zq = lambda n: jnp.zeros((MLA_Q_RANK, MLA_HEADS, n), F32)
    w_qb = jnp.concatenate(
        [jnp.concatenate([qb_nope, qb_rope, zq(32)], -1).reshape(MLA_Q_RANK, -1),
         jnp.concatenate([zq(64), qb_rope_rot, zq(32)], -1).reshape(MLA_Q_RANK, -1)],
        axis=-1).astype(BF16)

    kvb = mla_w_kvb[l].reshape(MLA_KV_RANK, MLA_HEADS, MLA_NOPE + MLA_V)
    w_kn = jnp.concatenate(
        [kvb[:, :, :MLA_NOPE], jnp.zeros((MLA_KV_RANK, MLA_HEADS, 64), F32)],
        -1).reshape(MLA_KV_RANK, -1).astype(BF16)
    w_v = kvb[:, :, MLA_NOPE:].reshape(MLA_KV_RANK, -1).astype(BF16)

    def swap16(g):
        return g.reshape(2, 2, 16)[:, ::-1, :].reshape(64)

    gqn, gkn = gqa_q_norm[l], gqa_k_norm[l]
    vecs = jnp.zeros((8, 256), F32)
    vecs = vecs.at[0, :].set(mla_q_norm[l])
    vecs = vecs.at[1, :128].set(mla_kv_norm[l])
    vecs = vecs.at[2, :128].set(jnp.tile(gqn, 2))
    vecs = vecs.at[3, :128].set(jnp.tile(swap16(gqn), 2))
    vecs = vecs.at[4, :128].set(jnp.tile(gkn, 2))
    vecs = vecs.at[5, :128].set(jnp.tile(swap16(gkn), 2))

    wo = w_out[l]
    wo_m = wo[:384].astype(BF16)
    wo_g = wo[384:768].reshape(GQA_HEADS, GQA_DIM, D_MODEL)[order, :, :]
    wo_g = wo_g.reshape(384, D_MODEL).astype(BF16)
    wo_d = wo[768:].astype(BF16)
    return dict(w_all=w_all, w_qb=w_qb, w_kn=w_kn, w_v=w_v, vecs=vecs,
                wo_m=wo_m, wo_g=wo_g, wo_d=wo_d)


def _prep_kernel(x_ref, ac_ref, as_ref, pc_ref, ps_ref, xc_ref, xs_ref,
                 wall_ref, wqb_ref, wkn_ref, wv_ref, vec_ref,
                 mq_ref, mk_ref, mv_ref, gq_ref, gk_ref, gv_ref, dq_ref, dk_ref, dv_ref):
    tm = x_ref.shape[0]
    xb = x_ref[...].astype(BF16)
    h = jnp.dot(xb, wall_ref[...], preferred_element_type=F32)
    seg = lambda name: h[:, _SEG[name][0]:_SEG[name][1]]
    a_c, a_s = ac_ref[...], as_ref[...]
    p_c, p_s = pc_ref[...], ps_ref[...]
    x_c, x_s = xc_ref[...], xs_ref[...]

    cq = seg('cq')
    cqn = cq * lax.rsqrt(jnp.mean(cq * cq, axis=-1, keepdims=True) + RMS_EPS)
    cqb = (cqn * vec_ref[0:1, :]).astype(BF16)
    q2 = jnp.dot(cqb, wqb_ref[...], preferred_element_type=F32)
    ckv = seg('ckv')
    ckvn = ckv * lax.rsqrt(jnp.mean(ckv * ckv, axis=-1, keepdims=True) + RMS_EPS)
    ckvb = (ckvn * vec_ref[1:2, 0:128]).astype(BF16)
    kn = jnp.dot(ckvb, wkn_ref[...], preferred_element_type=F32)
    kro = seg('kr') * a_c + seg('krr') * a_s
    for hd in range(MLA_HEADS):
        sl = slice(hd * LANES, (hd + 1) * LANES)
        sr = slice(768 + hd * LANES, 768 + (hd + 1) * LANES)
        mq_ref[:, sl] = ((q2[:, sl] * a_c + q2[:, sr] * a_s) * MLA_SCALE).astype(BF16)
        mk_ref[:, sl] = (kn[:, sl] + kro).astype(BF16)
    mv_ref[...] = jnp.dot(ckvb, wv_ref[...], preferred_element_type=F32).astype(BF16)

    lane = lax.broadcasted_iota(jnp.int32, (tm, LANES), 1)
    lo = lane < 64

    def half_rms(z):
        zz = z * z
        s_lo = jnp.sum(jnp.where(lo, zz, 0.0), axis=-1, keepdims=True)
        s_hi = jnp.sum(jnp.where(lo, 0.0, zz), axis=-1, keepdims=True)
        return jnp.where(lo, lax.rsqrt(s_lo * (1.0 / 64) + RMS_EPS),
                         lax.rsqrt(s_hi * (1.0 / 64) + RMS_EPS))

    gq, gqr = seg('gq'), seg('gqr')
    g_q, g_qs = vec_ref[2:3, 0:128], vec_ref[3:4, 0:128]
    for j in range(3):
        sl = slice(j * LANES, (j + 1) * LANES)
        z, zr = gq[:, sl], gqr[:, sl]
        r = half_rms(z)
        gq_ref[:, sl] = (r * ((z * g_q) * x_c + (zr * g_qs) * x_s) * GQA_SCALE).astype(BF16)
    z, zr = seg('gk'), seg('gkr')
    g_k, g_ks = vec_ref[4:5, 0:128], vec_ref[5:6, 0:128]
    gk_ref[...] = (half_rms(z) * ((z * g_k) * x_c + (zr * g_ks) * x_s)).astype(BF16)
    gv_ref[...] = seg('gv').astype(BF16)

    dq, dqr, dk, dkr = seg('dq'), seg('dqr'), seg('dk'), seg('dkr')
    for j in range(2):
        sl = slice(j * LANES, (j + 1) * LANES)
        dq_ref[:, sl] = ((dq[:, sl] * p_c + dqr[:, sl] * p_s) * DIFF_SCALE).astype(BF16)
        dk_ref[:, sl] = (dk[:, sl] * p_c + dkr[:, sl] * p_s).astype(BF16)
    dv_ref[...] = seg('dv').astype(BF16)


def _prep(x, tables, lw, seq):
    t = x.shape[0]
    tm = PREP_TM
    nblk = seq // tm
    row = lambda w: pl.BlockSpec((tm, w), lambda i: (i, 0))
    tab = pl.BlockSpec((tm, LANES), lambda i: (i % nblk, 0))
    full = lambda a: pl.BlockSpec(a.shape, lambda i: (0, 0))
    widths = (768, 768, 384, 384, 128, 128, 256, 256, 256)
    return pl.pallas_call(
        _prep_kernel,
        grid=(t // tm,),
        in_specs=[row(D_MODEL)] + [tab] * 6
        + [full(lw['w_all']), full(lw['w_qb']), full(lw['w_kn']), full(lw['w_v']),
           full(lw['vecs'])],
        out_specs=[row(w) for w in widths],
        out_shape=[jax.ShapeDtypeStruct((t, w), BF16) for w in widths],
        compiler_params=pltpu.CompilerParams(
            dimension_semantics=("parallel",), vmem_limit_bytes=VMEM_LIMIT),
        name="mixer_prep",
    )(x, *tables, lw['w_all'], lw['w_qb'], lw['w_kn'], lw['w_v'], lw['vecs'])


_NT = (((1,), (1,)), ((), ()))


def _softmax_parts(s):
    m = jnp.max(s, axis=-1, keepdims=True)
    e = jnp.exp(s - m)
    return e, jnp.sum(e, axis=-1, keepdims=True)


def _lane_mask(q, keep):
    return jnp.where(keep, q.astype(F32), 0.0).astype(BF16)


def _pair_attn_kernel(q_ref, k_ref, v_ref, o_ref, *, split_qk):
    q, k, v = q_ref[0], k_ref[0], v_ref[0]
    tq = q.shape[0]
    lane = lax.broadcasted_iota(jnp.int32, (tq, LANES), 1)
    lo = lane < 64
    if split_qk:
        qs = (q[:, :LANES], q[:, LANES:])
        ks = (k[:, :LANES], k[:, LANES:])
    else:
        qs = (_lane_mask(q, lo), _lane_mask(q, jnp.logical_not(lo)))
        ks = (k, k)
    outs = []
    for qa, ka in zip(qs, ks):
        s = lax.dot_general(qa, ka, _NT, preferred_element_type=F32)
        e, l = _softmax_parts(s)
        o = jnp.dot(e.astype(BF16), v, preferred_element_type=F32)
        outs.append(o * (1.0 / l))
    o_ref[0] = jnp.where(lo, outs[0], outs[1]).astype(o_ref.dtype)


def _diff_attn_kernel(lam_ref, gain_ref, q_ref, k_ref, v_ref, o_ref, *, lam_init):
    q, k, v = q_ref[0], k_ref[0], v_ref[0]
    tq = q.shape[0]
    lp = lam_ref[...]
    lam = (jnp.exp(jnp.sum(lp[0:1, :] * lp[1:2, :], axis=-1, keepdims=True))
           - jnp.exp(jnp.sum(lp[2:3, :] * lp[3:4, :], axis=-1, keepdims=True)) + lam_init)
    lane = lax.broadcasted_iota(jnp.int32, (tq, LANES), 1)
    lo = lane < 64
    outs = []
    for hd in range(2):
        m1 = (lane >= 64 * hd) & (lane < 64 * hd + 32)
        m2 = (lane >= 64 * hd + 32) & (lane < 64 * hd + 64)
        s1 = lax.dot_general(_lane_mask(q, m1), k, _NT, preferred_element_type=F32)
        e1, l1 = _softmax_parts(s1)
        s2 = lax.dot_general(_lane_mask(q, m2), k, _NT, preferred_element_type=F32)
        e2, l2 = _softmax_parts(s2)
        w = e1 * (1.0 / l1) - e2 * (lam / l2)
        o = jnp.dot(w.astype(BF16), v, preferred_element_type=F32)
        mine = lo if hd == 0 else jnp.logical_not(lo)
        ms = jnp.sum(jnp.where(mine, o * o, 0.0), axis=-1, keepdims=True) * (1.0 / DIFF_V)
        outs.append(o * lax.rsqrt(ms + RMS_EPS))
    o = jnp.where(lo, outs[0], outs[1]) * gain_ref[...] * (1.0 - lam_init)
    o_ref[0] = o.astype(o_ref.dtype)


def _attention(q, k, v, *, mode, lam_params=None, gain=None, lam_init=None):
    b, s, _ = q.shape
    tq = ATTN_TQ
    if mode == 'mla':
        npairs, qw, kw = 3, 2 * LANES, 2 * LANES
        kmap = lambda bi, p, i: (bi, 0, p)
        vmap = lambda bi, p, i: (bi, 0, p)
    elif mode == 'gqa':
        npairs, qw, kw = 3, LANES, LANES
        kmap = lambda bi, p, i: (bi, 0, 0)
        vmap = lambda bi, p, i: (bi, 0, 0)
    else:
        npairs, qw, kw = 2, LANES, LANES
        kmap = lambda bi, p, i: (bi, 0, p)
        vmap = lambda bi, p, i: (bi, 0, p)
    in_specs = [pl.BlockSpec((1, tq, qw), lambda bi, p, i: (bi, i, p)),
                pl.BlockSpec((1, s, kw), kmap),
                pl.BlockSpec((1, s, LANES), vmap)]
    args = [q, k, v]
    if mode == 'diff':
        body = functools.partial(_diff_attn_kernel, lam_init=lam_init)
        in_specs = [pl.BlockSpec(lam_params.shape, lambda bi, p, i: (0, 0)),
                    pl.BlockSpec(gain.shape, lambda bi, p, i: (0, 0))] + in_specs
        args = [lam_params, gain] + args
    else:
        body = functools.partial(_pair_attn_kernel, split_qk=(mode == 'mla'))
    return pl.pallas_call(
        body,
        grid=(b, npairs, s // tq),
        in_specs=in_specs,
        out_specs=pl.BlockSpec((1, tq, LANES), lambda bi, p, i: (bi, i, p)),
        out_shape=jax.ShapeDtypeStruct((b, s, npairs * LANES), BF16),
        compiler_params=pltpu.CompilerParams(
            dimension_semantics=("parallel", "parallel", "parallel"),
            vmem_limit_bytes=VMEM_LIMIT),
        name="attn_" + mode,
    )(*args)


def _layernorm(z, g, b):
    mu = jnp.mean(z, axis=-1, keepdims=True)
    zc = z - mu
    var = jnp.mean(zc * zc, axis=-1, keepdims=True)
    return zc * lax.rsqrt(var + LN_EPS) * g + b


def _outproj_kernel(x_ref, om_ref, og_ref, od_ref, wm_ref, wg_ref, wd_ref, g_ref, b_ref,
                    o_ref):
    y = (jnp.dot(om_ref[...], wm_ref[...], preferred_element_type=F32)
         + jnp.dot(og_ref[...], wg_ref[...], preferred_element_type=F32)
         + jnp.dot(od_ref[...], wd_ref[...], preferred_element_type=F32))
    o_ref[...] = _layernorm(ALPHA * x_ref[...] + y, g_ref[...], b_ref[...])


def _outproj(x, o_m, o_g, o_d, lw, g, b):
    t = x.shape[0]
    tm = PROJ_TM
    row = lambda w: pl.BlockSpec((tm, w), lambda i: (i, 0))
    full = lambda a: pl.BlockSpec(a.shape, lambda i: (0, 0))
    return pl.pallas_call(
        _outproj_kernel,
        grid=(t // tm,),
        in_specs=[row(D_MODEL), row(384), row(384), row(256),
                  full(lw['wo_m']), full(lw['wo_g']), full(lw['wo_d']), full(g), full(b)],
        out_specs=row(D_MODEL),
        out_shape=jax.ShapeDtypeStruct((t, D_MODEL), F32),
        compiler_params=pltpu.CompilerParams(
            dimension_semantics=("parallel",), vmem_limit_bytes=VMEM_LIMIT),
        name="outproj_ln",
    )(x, o_m, o_g, o_d, lw['wo_m'], lw['wo_g'], lw['wo_d'], g, b)


def _router_gates(x, wr_hi, wr_lo):
    x_hi = x.astype(BF16)
    x_lo = (x - x_hi.astype(F32)).astype(BF16)
    logits = (jnp.dot(x_hi, wr_hi, preferred_element_type=F32)
              + jnp.dot(x_lo, wr_hi, preferred_element_type=F32)
              + jnp.dot(x_hi, wr_lo, preferred_element_type=F32))
    lane = lax.broadcasted_iota(jnp.int32, logits.shape, 1)
    neg = -jnp.inf
    logits = jnp.where(lane < N_EXPERTS, logits, neg)
    m1 = jnp.max(logits, axis=-1, keepdims=True)
    i1 = jnp.min(jnp.where(logits == m1, lane, LANES), axis=-1, keepdims=True)
    rest = jnp.where(lane == i1, neg, logits)
    m2 = jnp.max(rest, axis=-1, keepdims=True)
    i2 = jnp.min(jnp.where(rest == m2, lane, LANES), axis=-1, keepdims=True)
    e2 = jnp.exp(m2 - m1)
    w1 = 1.0 / (1.0 + e2)
    return jnp.where(lane == i1, w1, 0.0) + jnp.where(lane == i2, e2 * w1, 0.0)


def _ffn_kernel(*refs, moe):
    if moe:
        (x_ref, rhi_ref, rlo_ref, wg_ref, wu_ref, wd_ref, g_ref, b_ref, o_ref,
         xb_sc, acc_sc, gate_sc) = refs
    else:
        x_ref, wg_ref, wu_ref, wd_ref, g_ref, b_ref, o_ref, xb_sc, acc_sc = refs
    e, f = pl.program_id(1), pl.program_id(2)
    first = jnp.logical_and(e == 0, f == 0)
    last = jnp.logical_and(e == pl.num_programs(1) - 1, f == pl.num_programs(2) - 1)

    @pl.when(first)
    def _():
        x = x_ref[...]
        xb_sc[...] = x.astype(BF16)
        acc_sc[...] = jnp.zeros_like(acc_sc)
        if moe:
            gate_sc[...] = _router_gates(x, rhi_ref[...], rlo_ref[...])

    xb = xb_sc[...]
    gt = jnp.dot(xb, wg_ref[0], preferred_element_type=F32)
    up = jnp.dot(xb, wu_ref[0], preferred_element_type=F32)
    act = (gt / (1.0 + jnp.exp(-gt)) * up).astype(BF16)
    y = jnp.dot(act, wd_ref[0], preferred_element_type=F32)
    if moe:
        gates = gate_sc[...]
        lane = lax.broadcasted_iota(jnp.int32, gates.shape, 1)
        y = y * jnp.sum(jnp.where(lane == e, gates, 0.0), axis=-1, keepdims=True)
    acc_sc[...] += y

    @pl.when(last)
    def _():
        o_ref[...] = _layernorm(ALPHA * x_ref[...] + acc_sc[...], g_ref[...], b_ref[...])


def _ffn(x, wg, wu, wd, g, b, router=None):
    t = x.shape[0]
    tm, tf = FFN_TM, FFN_TF
    n_e = wg.shape[0]
    moe = router is not None
    row = pl.BlockSpec((tm, D_MODEL), lambda i, e, f: (i, 0))
    full = lambda a: pl.BlockSpec(a.shape, lambda i, e, f: (0, 0))
    in_specs = [row]
    args = [x]
    scratch = [pltpu.VMEM((tm, D_MODEL), BF16), pltpu.VMEM((tm, D_MODEL), F32)]
    if moe:
        in_specs += [full(router[0]), full(router[1])]
        args += list(router)
        scratch.append(pltpu.VMEM((tm, LANES), F32))
    in_specs += [pl.BlockSpec((1, D_MODEL, tf), lambda i, e, f: (e, 0, f)),
                 pl.BlockSpec((1, D_MODEL, tf), lambda i, e, f: (e, 0, f)),
                 pl.BlockSpec((1, tf, D_MODEL), lambda i, e, f: (e, f, 0)),
                 full(g), full(b)]
    args += [wg, wu, wd, g, b]
    return pl.pallas_call(
        functools.partial(_ffn_kernel, moe=moe),
        grid=(t // tm, n_e, D_FF // tf),
        in_specs=in_specs,
        out_specs=row,
        out_shape=jax.ShapeDtypeStruct((t, D_MODEL), F32),
        scratch_shapes=scratch,
        compiler_params=pltpu.CompilerParams(
            dimension_semantics=("parallel", "arbitrary", "arbitrary"),
            vmem_limit_bytes=VMEM_LIMIT),
        name="moe_ffn_ln" if moe else "ffn_ln",
    )(*args)


def kernel(x_prompt, x_sample, w_in, mla_q_norm, mla_w_qb, mla_kv_norm, mla_w_kvb,
           gqa_q_norm, gqa_k_norm, diff_lambda, diff_out_norm, w_out, ln1_g, ln1_b,
           ffn_w_gate, ffn_w_up, ffn_w_down, moe_router, moe_w_gate, moe_w_up,
           moe_w_down, ln2_g, ln2_b):
    tables = _rope_tables()
    layers = [_layer_weights(l, w_in, mla_q_norm, mla_w_qb, mla_kv_norm, mla_w_kvb,
                             gqa_q_norm, gqa_k_norm, w_out) for l in range(DEPTH)]
    ffn_w = (ffn_w_gate.astype(BF16), ffn_w_up.astype(BF16), ffn_w_down.astype(BF16))
    moe_w = (moe_w_gate.astype(BF16), moe_w_up.astype(BF16), moe_w_down.astype(BF16))
    routers = []
    for m in range(moe_router.shape[0]):
        wr = jnp.pad(moe_router[m], ((0, 0), (0, LANES - N_EXPERTS)))
        wr_hi = wr.astype(BF16)
        routers.append((wr_hi, (wr - wr_hi.astype(F32)).astype(BF16)))

    def run(x):
        b, s, _ = x.shape
        t = b * s
        xf = x.reshape(t, D_MODEL)
        for l in range(DEPTH):
            lw = layers[l]
            mq, mk, mv, gq, gk, gv, dq, dk, dv = _prep(xf, tables, lw, s)
            r3 = lambda a: a.reshape(b, s, a.shape[-1])
            o_m = _attention(r3(mq), r3(mk), r3(mv), mode='mla')
            o_g = _attention(r3(gq), r3(gk), r3(gv), mode='gqa')
            lam_init = 0.8 - 0.6 * math.exp(-0.3 * l)
            o_d = _attention(r3(dq), r3(dk), r3(dv), mode='diff',
                             lam_params=diff_lambda[l].astype(F32),
                             gain=jnp.tile(diff_out_norm[l], 2).reshape(1, LANES),
                             lam_init=lam_init)
            x1 = _outproj(xf, o_m.reshape(t, -1), o_g.reshape(t, -1), o_d.reshape(t, -1),
                          lw, ln1_g[l].reshape(1, -1), ln1_b[l].reshape(1, -1))
            m = l // 2
            g2, b2 = ln2_g[l].reshape(1, -1), ln2_b[l].reshape(1, -1)
            if l % 2 == 0:
                xf = _ffn(x1, ffn_w[0][m:m + 1], ffn_w[1][m:m + 1], ffn_w[2][m:m + 1],
                          g2, b2)
            else:
                xf = _ffn(x1, moe_w[0][m], moe_w[1][m], moe_w[2][m], g2, b2,
                          router=routers[m])
        return xf.reshape(b, s, D_MODEL)

    return (run(x_prompt), run(x_sample))
```

```python
import functools
import math

import jax
import jax.numpy as jnp
from jax import lax
from jax.experimental import pallas as pl
from jax.experimental.pallas import tpu as pltpu

D_MODEL = 1024
DEPTH = 2
GRID_W = 64
ROPE_THETA = 10000.0
MLA_HEADS = 6
MLA_Q_RANK = 256
MLA_KV_RANK = 128
MLA_NOPE = 64
MLA_ROPE = 32
MLA_V = 64
GQA_HEADS = 6
GQA_KV_HEADS = 2
GQA_DIM = 64
DIFF_HEADS = 4
DIFF_QK = 32
DIFF_V = 64
D_FF = 2816
N_EXPERTS = 8
ALPHA = (2 * DEPTH) ** 0.25
LN_EPS = 1e-5
RMS_EPS = 1e-6

MLA_SCALE = (MLA_NOPE + MLA_ROPE) ** -0.5
GQA_SCALE = GQA_DIM ** -0.5
DIFF_SCALE = DIFF_QK ** -0.5

LANES = 128
VMEM_LIMIT = 56 * 1024 * 1024
MAX_SEQ = 4096

PREP_TM = 256
ATTN_TQ = 256
PROJ_TM = 512
FFN_TM = 512
FFN_TF = 1408
COMBINE_TM = 256
DMA_UNROLL = 8

BF16 = jnp.bfloat16
F32 = jnp.float32

_SEG = dict(cq=(0, 256), ckv=(256, 384), kr=(384, 512), krr=(512, 640),
            gq=(640, 1024), gqr=(1024, 1408), gk=(1408, 1536), gkr=(1536, 1664),
            gv=(1664, 1792), dq=(1792, 2048), dqr=(2048, 2304), dk=(2304, 2560),
            dkr=(2560, 2816), dv=(2816, 3072))
PREP_N = 3072
GQA_ORDER = (0, 3, 1, 4, 2, 5)


def _rot_cols(w, half=16):
    k, n = w.shape
    w4 = w.reshape(k, n // (2 * half), 2, half)
    return jnp.stack([-w4[:, :, 1, :], w4[:, :, 0, :]], axis=2).reshape(k, n)


def _rope_tables():
    pos = jnp.arange(MAX_SEQ, dtype=F32)
    inv = ROPE_THETA ** (-jnp.arange(0, 32, 2, dtype=F32) / 32)

    def cs(p):
        ang = p[:, None] * inv[None, :]
        c, s = jnp.cos(ang), jnp.sin(ang)
        return jnp.concatenate([c, c], -1), jnp.concatenate([s, s], -1)

    c32, s32 = cs(pos)
    row = jnp.floor(pos / GRID_W)
    col = pos - row * GRID_W
    cr, sr = cs(row)
    cc, sc = cs(col)
    ones = jnp.ones((MAX_SEQ, 64), F32)
    z64 = jnp.zeros((MAX_SEQ, 64), F32)
    z32 = jnp.zeros((MAX_SEQ, 32), F32)
    a_c = jnp.concatenate([ones, c32, z32], -1)
    a_s = jnp.concatenate([z64, s32, z32], -1)
    p_c = jnp.tile(c32, (1, 4))
    p_s = jnp.tile(s32, (1, 4))
    x_c = jnp.tile(jnp.concatenate([cr, cc], -1), (1, 2))
    x_s = jnp.tile(jnp.concatenate([sr, sc], -1), (1, 2))
    return a_c, a_s, p_c, p_s, x_c, x_s


def _layer_weights(l, w_in, mla_q_norm, mla_w_qb, mla_kv_norm, mla_w_kvb, gqa_q_norm,
                   gqa_k_norm, w_out):
    w = w_in[l]
    o = [0, 256, 384, 416, 800, 928, 1056, 1312, 1568, 1824]
    cq, ckv, kr, gq, gk, gv, dq, dk, dv = [w[:, o[i]:o[i + 1]] for i in range(9)]
    z = lambda n: jnp.zeros((D_MODEL, n), F32)
    kr_p = jnp.concatenate([z(64), kr, z(32)], -1)
    krr_p = jnp.concatenate([z(64), _rot_cols(kr), z(32)], -1)
    order = jnp.asarray(GQA_ORDER)
    gq_p = gq.reshape(D_MODEL, GQA_HEADS, GQA_DIM)[:, order, :].reshape(D_MODEL, -1)
    w_all = jnp.concatenate(
        [cq, ckv, kr_p, krr_p, gq_p, _rot_cols(gq_p), gk, _rot_cols(gk), gv,
         dq, _rot_cols(dq), dk, _rot_cols(dk), dv], axis=-1).astype(BF16)

    qb = mla_w_qb[l].reshape(MLA_Q_RANK, MLA_HEADS, MLA_NOPE + MLA_ROPE)
    qb_nope, qb_rope = qb[:, :, :MLA_NOPE], qb[:, :, MLA_NOPE:]
    qb_rope_rot = _rot_cols(qb_rope.reshape(MLA_Q_RANK, -1)).reshape(qb_rope.shape)
    zq = lambda n: jnp.zeros((MLA_Q_RANK, MLA_HEADS, n), F32)
    w_qb = jnp.concatenate(
        [jnp.concatenate([qb_nope, qb_rope, zq(32)], -1).reshape(MLA_Q_RANK, -1),
         jnp.concatenate([zq(64), qb_rope_rot, zq(32)], -1).reshape(MLA_Q_RANK, -1)],
        axis=-1).astype(BF16)

    kvb = mla_w_kvb[l].reshape(MLA_KV_RANK, MLA_HEADS, MLA_NOPE + MLA_V)
    w_kn = jnp.concatenate(
        [kvb[:, :, :MLA_NOPE], jnp.zeros((MLA_KV_RANK, MLA_HEADS, 64), F32)],
        -1).reshape(MLA_KV_RANK, -1).astype(BF16)
    w_v = kvb[:, :, MLA_NOPE:].reshape(MLA_KV_RANK, -1).astype(BF16)

    def swap16(g):
        return g.reshape(2, 2, 16)[:, ::-1, :].reshape(64)

    gqn, gkn = gqa_q_norm[l], gqa_k_norm[l]
    vecs = jnp.zeros((8, 256), F32)
    vecs = vecs.at[0, :].set(mla_q_norm[l])
    vecs = vecs.at[1, :128].set(mla_kv_norm[l])
    vecs = vecs.at[2, :128].set(jnp.tile(gqn, 2))
    vecs = vecs.at[3, :128].set(jnp.tile(swap16(gqn), 2))
    vecs = vecs.at[4, :128].set(jnp.tile(gkn, 2))
    vecs = vecs.at[5, :128].set(jnp.tile(swap16(gkn), 2))

    wo = w_out[l]
    wo_m = wo[:384].astype(BF16)
    wo_g = wo[384:768].reshape(GQA_HEADS, GQA_DIM, D_MODEL)[order, :, :]
    wo_g = wo_g.reshape(384, D_MODEL).astype(BF16)
    wo_d = wo[768:].astype(BF16)
    return dict(w_all=w_all, w_qb=w_qb, w_kn=w_kn, w_v=w_v, vecs=vecs,
                wo_m=wo_m, wo_g=wo_g, wo_d=wo_d)


def _prep_kernel(x_ref, ac_ref, as_ref, pc_ref, ps_ref, xc_ref, xs_ref,
                 wall_ref, wqb_ref, wkn_ref, wv_ref, vec_ref,
                 mq_ref, mk_ref, mv_ref, gq_ref, gk_ref, gv_ref, dq_ref, dk_ref, dv_ref):
    tm = x_ref.shape[0]
    xb = x_ref[...].astype(BF16)
    h = jnp.dot(xb, wall_ref[...], preferred_element_type=F32)
    seg = lambda name: h[:, _SEG[name][0]:_SEG[name][1]]
    a_c, a_s = ac_ref[...], as_ref[...]
    p_c, p_s = pc_ref[...], ps_ref[...]
    x_c, x_s = xc_ref[...], xs_ref[...]

    cq = seg('cq')
    cqn = cq * lax.rsqrt(jnp.mean(cq * cq, axis=-1, keepdims=True) + RMS_EPS)
    cqb = (cqn * vec_ref[0:1, :]).astype(BF16)
    q2 = jnp.dot(cqb, wqb_ref[...], preferred_element_type=F32)
    ckv = seg('ckv')
    ckvn = ckv * lax.rsqrt(jnp.mean(ckv * ckv, axis=-1, keepdims=True) + RMS_EPS)
    ckvb = (ckvn * vec_ref[1:2, 0:128]).astype(BF16)
    kn = jnp.dot(ckvb, wkn_ref[...], preferred_element_type=F32)
    kro = seg('kr') * a_c + seg('krr') * a_s
    for hd in range(MLA_HEADS):
        sl = slice(hd * LANES, (hd + 1) * LANES)
        sr = slice(768 + hd * LANES, 768 + (hd + 1) * LANES)
        mq_ref[:, sl] = ((q2[:, sl] * a_c + q2[:, sr] * a_s) * MLA_SCALE).astype(BF16)
        mk_ref[:, sl] = (kn[:, sl] + kro).astype(BF16)
    mv_ref[...] = jnp.dot(ckvb, wv_ref[...], preferred_element_type=F32).astype(BF16)

    lane = lax.broadcasted_iota(jnp.int32, (tm, LANES), 1)
    lo = lane < 64

    def half_rms(z):
        zz = z * z
        s_lo = jnp.sum(jnp.where(lo, zz, 0.0), axis=-1, keepdims=True)
        s_hi = jnp.sum(jnp.where(lo, 0.0, zz), axis=-1, keepdims=True)
        return jnp.where(lo, lax.rsqrt(s_lo * (1.0 / 64) + RMS_EPS),
                         lax.rsqrt(s_hi * (1.0 / 64) + RMS_EPS))

    gq, gqr = seg('gq'), seg('gqr')
    g_q, g_qs = vec_ref[2:3, 0:128], vec_ref[3:4, 0:128]
    for j in range(3):
        sl = slice(j * LANES, (j + 1) * LANES)
        z, zr = gq[:, sl], gqr[:, sl]
        r = half_rms(z)
        gq_ref[:, sl] = (r * ((z * g_q) * x_c + (zr * g_qs) * x_s) * GQA_SCALE).astype(BF16)
    z, zr = seg('gk'), seg('gkr')
    g_k, g_ks = vec_ref[4:5, 0:128], vec_ref[5:6, 0:128]
    gk_ref[...] = (half_rms(z) * ((z * g_k) * x_c + (zr * g_ks) * x_s)).astype(BF16)
    gv_ref[...] = seg('gv').astype(BF16)

    dq, dqr, dk, dkr = seg('dq'), seg('dqr'), seg('dk'), seg('dkr')
    for j in range(2):
        sl = slice(j * LANES, (j + 1) * LANES)
        dq_ref[:, sl] = ((dq[:, sl] * p_c + dqr[:, sl] * p_s) * DIFF_SCALE).astype(BF16)
        dk_ref[:, sl] = (dk[:, sl] * p_c + dkr[:, sl] * p_s).astype(BF16)
    dv_ref[...] = seg('dv').astype(BF16)


def _prep(x, tables, lw, seq):
    t = x.shape[0]
    tm = PREP_TM
    nblk = seq // tm
    row = lambda w: pl.BlockSpec((tm, w), lambda i: (i, 0))
    tab = pl.BlockSpec((tm, LANES), lambda i: (i % nblk, 0))
    full = lambda a: pl.BlockSpec(a.shape, lambda i: (0, 0))
    widths = (768, 768, 384, 384, 128, 128, 256, 256, 256)
    return pl.pallas_call(
        _prep_kernel,
        grid=(t // tm,),
        in_specs=[row(D_MODEL)] + [tab] * 6
        + [full(lw['w_all']), full(lw['w_qb']), full(lw['w_kn']), full(lw['w_v']),
           full(lw['vecs'])],
        out_specs=[row(w) for w in widths],
        out_shape=[jax.ShapeDtypeStruct((t, w), BF16) for w in widths],
        compiler_params=pltpu.CompilerParams(
            dimension_semantics=("parallel",), vmem_limit_bytes=VMEM_LIMIT),
        name="mixer_prep",
    )(x, *tables, lw['w_all'], lw['w_qb'], lw['w_kn'], lw['w_v'], lw['vecs'])


_NT = (((1,), (1,)), ((), ()))


def _softmax_parts(s):
    m = jnp.max(s, axis=-1, keepdims=True)
    e = jnp.exp(s - m)
    return e, jnp.sum(e, axis=-1, keepdims=True)


def _lane_mask(q, keep):
    return jnp.where(keep, q.astype(F32), 0.0).astype(BF16)


def _pair_attn_kernel(q_ref, k_ref, v_ref, o_ref, *, split_qk):
    q, k, v = q_ref[0], k_ref[0], v_ref[0]
    tq = q.shape[0]
    lane = lax.broadcasted_iota(jnp.int32, (tq, LANES), 1)
    lo = lane < 64
    if split_qk:
        qs = (q[:, :LANES], q[:, LANES:])
        ks = (k[:, :LANES], k[:, LANES:])
    else:
        qs = (_lane_mask(q, lo), _lane_mask(q, jnp.logical_not(lo)))
        ks = (k, k)
    outs = []
    for qa, ka in zip(qs, ks):
        s = lax.dot_general(qa, ka, _NT, preferred_element_type=F32)
        e, l = _softmax_parts(s)
        o = jnp.dot(e.astype(BF16), v, preferred_element_type=F32)
        outs.append(o * (1.0 / l))
    o_ref[0] = jnp.where(lo, outs[0], outs[1]).astype(o_ref.dtype)


def _diff_attn_kernel(lam_ref, gain_ref, q_ref, k_ref, v_ref, o_ref, *, lam_init):
    q, k, v = q_ref[0], k_ref[0], v_ref[0]
    tq = q.shape[0]
    lp = lam_ref[...]
    lam = (jnp.exp(jnp.sum(lp[0:1, :] * lp[1:2, :], axis=-1, keepdims=True))
           - jnp.exp(jnp.sum(lp[2:3, :] * lp[3:4, :], axis=-1, keepdims=True)) + lam_init)
    lane = lax.broadcasted_iota(jnp.int32, (tq, LANES), 1)
    lo = lane < 64
    outs = []
    for hd in range(2):
        m1 = (lane >= 64 * hd) & (lane < 64 * hd + 32)
        m2 = (lane >= 64 * hd + 32) & (lane < 64 * hd + 64)
        s1 = lax.dot_general(_lane_mask(q, m1), k, _NT, preferred_element_type=F32)
        e1, l1 = _softmax_parts(s1)
        s2 = lax.dot_general(_lane_mask(q, m2), k, _NT, preferred_element_type=F32)
        e2, l2 = _softmax_parts(s2)
        w = e1 * (1.0 / l1) - e2 * (lam / l2)
        o = jnp.dot(w.astype(BF16), v, preferred_element_type=F32)
        mine = lo if hd == 0 else jnp.logical_not(lo)
        ms = jnp.sum(jnp.where(mine, o * o, 0.0), axis=-1, keepdims=True) * (1.0 / DIFF_V)
        outs.append(o * lax.rsqrt(ms + RMS_EPS))
    o = jnp.where(lo, outs[0], outs[1]) * gain_ref[...] * (1.0 - lam_init)
    o_ref[0] = o.astype(o_ref.dtype)


def _attention(q, k, v, *, mode, lam_params=None, gain=None, lam_init=None):
    b, s, _ = q.shape
    tq = ATTN_TQ
    if mode == 'mla':
        npairs, qw, kw = 3, 2 * LANES, 2 * LANES
        kmap = lambda bi, p, i: (bi, 0, p)
        vmap = lambda bi, p, i: (bi, 0, p)
    elif mode == 'gqa':
        npairs, qw, kw = 3, LANES, LANES
        kmap = lambda bi, p, i: (bi, 0, 0)
        vmap = lambda bi, p, i: (bi, 0, 0)
    else:
        npairs, qw, kw = 2, LANES, LANES
        kmap = lambda bi, p, i: (bi, 0, p)
        vmap = lambda bi, p, i: (bi, 0, p)
    in_specs = [pl.BlockSpec((1, tq, qw), lambda bi, p, i: (bi, i, p)),
                pl.BlockSpec((1, s, kw), kmap),
                pl.BlockSpec((1, s, LANES), vmap)]
    args = [q, k, v]
    if mode == 'diff':
        body = functools.partial(_diff_attn_kernel, lam_init=lam_init)
        in_specs = [pl.BlockSpec(lam_params.shape, lambda bi, p, i: (0, 0)),
                    pl.BlockSpec(gain.shape, lambda bi, p, i: (0, 0))] + in_specs
        args = [lam_params, gain] + args
    else:
        body = functools.partial(_pair_attn_kernel, split_qk=(mode == 'mla'))
    return pl.pallas_call(
        body,
        grid=(b, npairs, s // tq),
        in_specs=in_specs,
        out_specs=pl.BlockSpec((1, tq, LANES), lambda bi, p, i: (bi, i, p)),
        out_shape=jax.ShapeDtypeStruct((b, s, npairs * LANES), BF16),
        compiler_params=pltpu.CompilerParams(
            dimension_semantics=("parallel", "parallel", "parallel"),
            vmem_limit_bytes=VMEM_LIMIT),
        name="attn_" + mode,
    )(*args)


def _layernorm(z, g, b):
    mu = jnp.mean(z, axis=-1, keepdims=True)
    zc = z - mu
    var = jnp.mean(zc * zc, axis=-1, keepdims=True)
    return zc * lax.rsqrt(var + LN_EPS) * g + b


def _router_top2(x, wr_hi, wr_lo):
    x_hi = x.astype(BF16)
    x_lo = (x - x_hi.astype(F32)).astype(BF16)
    logits = (jnp.dot(x_hi, wr_hi, preferred_element_type=F32)
              + jnp.dot(x_lo, wr_hi, preferred_element_type=F32)
              + jnp.dot(x_hi, wr_lo, preferred_element_type=F32))
    lane = lax.broadcasted_iota(jnp.int32, logits.shape, 1)
    neg = -jnp.inf
    logits = jnp.where(lane < N_EXPERTS, logits, neg)
    m1 = jnp.max(logits, axis=-1, keepdims=True)
    i1 = jnp.min(jnp.where(logits == m1, lane, LANES), axis=-1, keepdims=True)
    rest = jnp.where(lane == i1, neg, logits)
    m2 = jnp.max(rest, axis=-1, keepdims=True)
    i2 = jnp.min(jnp.where(rest == m2, lane, LANES), axis=-1, keepdims=True)
    e2 = jnp.exp(m2 - m1)
    w1 = 1.0 / (1.0 + e2)
    return i1, i2, w1, e2 * w1


def _outproj_kernel(*refs, route):
    if route:
        (x_ref, om_ref, og_ref, od_ref, wm_ref, wg_ref, wd_ref, g_ref, b_ref,
         rhi_ref, rlo_ref, o_ref, ri_ref, rw_ref) = refs
    else:
        (x_ref, om_ref, og_ref, od_ref, wm_ref, wg_ref, wd_ref, g_ref, b_ref,
         o_ref) = refs
    y = (jnp.dot(om_ref[...], wm_ref[...], preferred_element_type=F32)
         + jnp.dot(og_ref[...], wg_ref[...], preferred_element_type=F32)
         + jnp.dot(od_ref[...], wd_ref[...], preferred_element_type=F32))
    x1 = _layernorm(ALPHA * x_ref[...] + y, g_ref[...], b_ref[...])
    o_ref[...] = x1
    if route:
        i1, i2, w1, w2 = _router_top2(x1, rhi_ref[...], rlo_ref[...])
        lane = lax.broadcasted_iota(jnp.int32, ri_ref.shape, 1)
        ri_ref[...] = jnp.where(lane == 0, i1, jnp.where(lane == 1, i2, 0))
        rw_ref[...] = jnp.where(lane == 0, w1, jnp.where(lane == 1, w2, 0.0))


def _outproj(x, o_m, o_g, o_d, lw, g, b, router=None):
    t = x.shape[0]
    tm = PROJ_TM
    route = router is not None
    row = lambda w: pl.BlockSpec((tm, w), lambda i: (i, 0))
    full = lambda a: pl.BlockSpec(a.shape, lambda i: (0, 0))
    in_specs = [row(D_MODEL), row(384), row(384), row(256),
                full(lw['wo_m']), full(lw['wo_g']), full(lw['wo_d']), full(g), full(b)]
    args = [x, o_m, o_g, o_d, lw['wo_m'], lw['wo_g'], lw['wo_d'], g, b]
    out_specs = [row(D_MODEL)]
    out_shape = [jax.ShapeDtypeStruct((t, D_MODEL), F32)]
    if route:
        in_specs += [full(router[0]), full(router[1])]
        args += list(router)
        out_specs += [row(LANES), row(LANES)]
        out_shape += [jax.ShapeDtypeStruct((t, LANES), jnp.int32),
                      jax.ShapeDtypeStruct((t, LANES), F32)]
    outs = pl.pallas_call(
        functools.partial(_outproj_kernel, route=route),
        grid=(t // tm,),
        in_specs=in_specs,
        out_specs=out_specs,
        out_shape=out_shape,
        compiler_params=pltpu.CompilerParams(
            dimension_semantics=("parallel",), vmem_limit_bytes=VMEM_LIMIT),
        name="outproj_ln_route" if route else "outproj_ln",
    )(*args)
    return outs if route else outs[0]


def _swiglu_block(xb, wg, wu, wd):
    gt = jnp.dot(xb, wg, preferred_element_type=F32)
    up = jnp.dot(xb, wu, preferred_element_type=F32)
    act = (gt / (1.0 + jnp.exp(-gt)) * up).astype(BF16)
    return jnp.dot(act, wd, preferred_element_type=F32)


def _ffn_kernel(x_ref, wg_ref, wu_ref, wd_ref, g_ref, b_ref, o_ref, xb_sc, acc_sc):
    f = pl.program_id(1)

    @pl.when(f == 0)
    def _():
        xb_sc[...] = x_ref[...].astype(BF16)
        acc_sc[...] = jnp.zeros_like(acc_sc)

    acc_sc[...] += _swiglu_block(xb_sc[...], wg_ref[...], wu_ref[...], wd_ref[...])

    @pl.when(f == pl.num_programs(1) - 1)
    def _():
        o_ref[...] = _layernorm(ALPHA * x_ref[...] + acc_sc[...], g_ref[...], b_ref[...])


def _ffn(x, wg, wu, wd, g, b):
    t = x.shape[0]
    tm, tf = FFN_TM, FFN_TF
    row = pl.BlockSpec((tm, D_MODEL), lambda i, f: (i, 0))
    full = lambda a: pl.BlockSpec(a.shape, lambda i, f: (0, 0))
    return pl.pallas_call(
        _ffn_kernel,
        grid=(t // tm, D_FF // tf),
        in_specs=[row,
                  pl.BlockSpec((D_MODEL, tf), lambda i, f: (0, f)),
                  pl.BlockSpec((D_MODEL, tf), lambda i, f: (0, f)),
                  pl.BlockSpec((tf, D_MODEL), lambda i, f: (f, 0)),
                  full(g), full(b)],
        out_specs=row,
        out_shape=jax.ShapeDtypeStruct((t, D_MODEL), F32),
        scratch_shapes=[pltpu.VMEM((tm, D_MODEL), BF16), pltpu.VMEM((tm, D_MODEL), F32)],
        compiler_params=pltpu.CompilerParams(
            dimension_semantics=("parallel", "arbitrary"), vmem_limit_bytes=VMEM_LIMIT),
        name="ffn_ln",
    )(x, wg, wu, wd, g, b)


def _route_plan(ri, tm):
    ea = ri[:, :2].reshape(-1)
    n_assign = ea.shape[0]
    experts = jnp.arange(N_EXPERTS, dtype=jnp.int32)
    oh = (ea[:, None] == experts[None, :]).astype(jnp.int32)
    csum = jnp.cumsum(oh, axis=0)
    rank = jnp.sum((csum - oh) * oh, axis=1)
    tiles = (csum[-1] + tm - 1) // tm
    tile_end = jnp.cumsum(tiles)
    start = (tile_end - tiles) * tm
    dest = jnp.sum(oh * start[None, :], axis=1) + rank
    n_tiles = n_assign // tm + N_EXPERTS
    tile_ids = jnp.arange(n_tiles, dtype=jnp.int32)
    tile_expert = jnp.sum((tile_ids[:, None] >= tile_end[None, :]).astype(jnp.int32), axis=1)
    last_used = jnp.max(jnp.where(tiles > 0, experts, 0))
    tile_expert = jnp.minimum(tile_expert, last_used).astype(jnp.int32)
    src = jnp.zeros((n_tiles * tm,), jnp.int32).at[dest].set(
        jnp.arange(n_assign, dtype=jnp.int32) // 2)
    return dict(dest=dest.astype(jnp.int32), src=src, tile_expert=tile_expert,
                n_valid=tile_end[-1:].astype(jnp.int32), n_tiles=n_tiles)


def _row_copy(src_hbm, src_row, dst, dst_row, sem):
    return pltpu.make_async_copy(src_hbm.at[pl.ds(src_row, 1)], dst.at[pl.ds(dst_row, 1)],
                                 sem.at[0])


def _row_gather_kernel(src_ref, x_hbm, o_hbm, sem):
    tm = src_ref.shape[-1]
    base = pl.program_id(0) * tm

    def issue(r, c):
        _row_copy(x_hbm, src_ref[0, 0, r], o_hbm, base + r, sem).start()
        return c

    lax.fori_loop(0, tm, issue, 0, unroll=DMA_UNROLL)
    pltpu.make_async_copy(x_hbm.at[pl.ds(0, tm)], o_hbm.at[pl.ds(base, tm)], sem.at[0]).wait()


def _row_gather(src, x, tm):
    n = src.shape[0]
    return pl.pallas_call(
        _row_gather_kernel,
        grid=(n // tm,),
        in_specs=[pl.BlockSpec((1, 1, tm), lambda i: (i, 0, 0), memory_space=pltpu.SMEM),
                  pl.BlockSpec(memory_space=pl.ANY)],
        out_specs=pl.BlockSpec(memory_space=pl.ANY),
        out_shape=jax.ShapeDtypeStruct((n, x.shape[1]), x.dtype),
        scratch_shapes=[pltpu.SemaphoreType.DMA((1,))],
        compiler_params=pltpu.CompilerParams(dimension_semantics=("arbitrary",)),
        name="moe_row_gather",
    )(src.reshape(n // tm, 1, tm), x)


def _grouped_ffn_kernel(te_ref, nv_ref, x_ref, wg_ref, wu_ref, wd_ref, o_ref, xb_sc, acc_sc):
    i, f = pl.program_id(0), pl.program_id(1)
    last = f == pl.num_programs(1) - 1
    valid = i < nv_ref[0]

    @pl.when(valid)
    def _():
        @pl.when(f == 0)
        def _():
            xb_sc[...] = x_ref[...].astype(BF16)
            acc_sc[...] = jnp.zeros_like(acc_sc)

        acc_sc[...] += _swiglu_block(xb_sc[...], wg_ref[0], wu_ref[0], wd_ref[0])

        @pl.when(last)
        def _():
            o_ref[...] = acc_sc[...]

    @pl.when(jnp.logical_and(jnp.logical_not(valid), last))
    def _():
        o_ref[...] = jnp.zeros_like(o_ref)


def _grouped_ffn(plan, xs, wg, wu, wd, tm):
    tf = FFN_TF
    row = pl.BlockSpec((tm, D_MODEL), lambda i, f, te, nv: (i, 0))
    return pl.pallas_call(
        _grouped_ffn_kernel,
        grid_spec=pltpu.PrefetchScalarGridSpec(
            num_scalar_prefetch=2,
            grid=(plan['n_tiles'], D_FF // tf),
            in_specs=[row,
                      pl.BlockSpec((1, D_MODEL, tf), lambda i, f, te, nv: (te[i], 0, f)),
                      pl.BlockSpec((1, D_MODEL, tf), lambda i, f, te, nv: (te[i], 0, f)),
                      pl.BlockSpec((1, tf, D_MODEL), lambda i, f, te, nv: (te[i], f, 0))],
            out_specs=row,
            scratch_shapes=[pltpu.VMEM((tm, D_MODEL), BF16),
                            pltpu.VMEM((tm, D_MODEL), F32)]),
        out_shape=jax.ShapeDtypeStruct(xs.shape, F32),
        compiler_params=pltpu.CompilerParams(
            dimension_semantics=("parallel", "arbitrary"), vmem_limit_bytes=VMEM_LIMIT),
        name="moe_grouped_ffn",
    )(plan['tile_expert'], plan['n_valid'], xs, wg, wu, wd)


def _combine_kernel(dest_ref, x_ref, w_ref, ys_hbm, g_ref, b_ref, o_ref, ybuf, sem):
    tm = x_ref.shape[0]

    def issue(j, c):
        _row_copy(ys_hbm, dest_ref[0, 0, j], ybuf, j, sem).start()
        return c

    lax.fori_loop(0, 2 * tm, issue, 0, unroll=DMA_UNROLL)
    pltpu.make_async_copy(ys_hbm.at[pl.ds(0, 2 * tm)], ybuf, sem.at[0]).wait()
    w = w_ref[...]
    y = w[:, 0:1] * ybuf[0:tm, :] + w[:, 1:2] * ybuf[tm:2 * tm, :]
    o_ref[...] = _layernorm(ALPHA * x_ref[...] + y, g_ref[...], b_ref[...])


def _combine(plan, x, rw, ys, g, b):
    t = x.shape[0]
    tm = COMBINE_TM
    dest = plan['dest'].reshape(t // tm, tm, 2).transpose(0, 2, 1).reshape(t // tm, 1, 2 * tm)
    full = lambda a: pl.BlockSpec(a.shape, lambda i: (0, 0))
    return pl.pallas_call(
        _combine_kernel,
        grid=(t // tm,),
        in_specs=[pl.BlockSpec((1, 1, 2 * tm), lambda i: (i, 0, 0), memory_space=pltpu.SMEM),
                  pl.BlockSpec((tm, D_MODEL), lambda i: (i, 0)),
                  pl.BlockSpec((tm, LANES), lambda i: (i, 0)),
                  pl.BlockSpec(memory_space=pl.ANY), full(g), full(b)],
        out_specs=pl.BlockSpec((tm, D_MODEL), lambda i: (i, 0)),
        out_shape=jax.ShapeDtypeStruct((t, D_MODEL), F32),
        scratch_shapes=[pltpu.VMEM((2 * tm, D_MODEL), F32), pltpu.SemaphoreType.DMA((1,))],
        compiler_params=pltpu.CompilerParams(dimension_semantics=("arbitrary",)),
        name="moe_combine_ln",
    )(dest, x, rw, ys, g, b)


def kernel(x_prompt, x_sample, w_in, mla_q_norm, mla_w_qb, mla_kv_norm, mla_w_kvb,
           gqa_q_norm, gqa_k_norm, diff_lambda, diff_out_norm, w_out, ln1_g, ln1_b,
           ffn_w_gate, ffn_w_up, ffn_w_down, moe_router, moe_w_gate, moe_w_up,
           moe_w_down, ln2_g, ln2_b):
    tables = _rope_tables()
    layers = [_layer_weights(l, w_in, mla_q_norm, mla_w_qb, mla_kv_norm, mla_w_kvb,
                             gqa_q_norm, gqa_k_norm, w_out) for l in range(DEPTH)]
    ffn_w = (ffn_w_gate.astype(BF16), ffn_w_up.astype(BF16), ffn_w_down.astype(BF16))
    moe_w = (moe_w_gate.astype(BF16), moe_w_up.astype(BF16), moe_w_down.astype(BF16))
    routers = []
    for m in range(moe_router.shape[0]):
        wr = jnp.pad(moe_router[m], ((0, 0), (0, LANES - N_EXPERTS)))
        wr_hi = wr.astype(BF16)
        routers.append((wr_hi, (wr - wr_hi.astype(F32)).astype(BF16)))

    def run(x):
        b, s, _ = x.shape
        t = b * s
        xf = x.reshape(t, D_MODEL)
        for l in range(DEPTH):
            lw = layers[l]
            mq, mk, mv, gq, gk, gv, dq, dk, dv = _prep(xf, tables, lw, s)
            r3 = lambda a: a.reshape(b, s, a.shape[-1])
            o_m = _attention(r3(mq), r3(mk), r3(mv), mode='mla')
            o_g = _attention(r3(gq), r3(gk), r3(gv), mode='gqa')
            lam_init = 0.8 - 0.6 * math.exp(-0.3 * l)
            o_d = _attention(r3(dq), r3(dk), r3(dv), mode='diff',
                             lam_params=diff_lambda[l].astype(F32),
                             gain=jnp.tile(diff_out_norm[l], 2).reshape(1, LANES),
                             lam_init=lam_init)
            mix = (o_m.reshape(t, -1), o_g.reshape(t, -1), o_d.reshape(t, -1))
            g1, b1 = ln1_g[l].reshape(1, -1), ln1_b[l].reshape(1, -1)
            g2, b2 = ln2_g[l].reshape(1, -1), ln2_b[l].reshape(1, -1)
            m = l // 2
            if l % 2 == 0:
                x1 = _outproj(xf, *mix, lw, g1, b1)
                xf = _ffn(x1, ffn_w[0][m], ffn_w[1][m], ffn_w[2][m], g2, b2)
            else:
                x1, ri, rw = _outproj(xf, *mix, lw, g1, b1, router=routers[m])
                plan = _route_plan(ri, FFN_TM)
                xs = _row_gather(plan['src'], x1, FFN_TM)
                ys = _grouped_ffn(plan, xs, moe_w[0][m], moe_w[1][m], moe_w[2][m], FFN_TM)
                xf = _combine(plan, x1, rw, ys, g2, b2)
        return xf.reshape(b, s, D_MODEL)

    return (run(x_prompt), run(x_sample))
```

```python
import functools
import math

import jax
import jax.numpy as jnp
from jax import lax
from jax.experimental import pallas as pl
from jax.experimental.pallas import tpu as pltpu

D_MODEL = 1024
DEPTH = 2
GRID_W = 64
ROPE_THETA = 10000.0
MLA_HEADS = 6
MLA_Q_RANK = 256
MLA_KV_RANK = 128
MLA_NOPE = 64
MLA_ROPE = 32
MLA_V = 64
GQA_HEADS = 6
GQA_KV_HEADS = 2
GQA_DIM = 64
DIFF_HEADS = 4
DIFF_QK = 32
DIFF_V = 64
D_FF = 2816
N_EXPERTS = 8
ALPHA = (2 * DEPTH) ** 0.25
LN_EPS = 1e-5
RMS_EPS = 1e-6

MLA_SCALE = (MLA_NOPE + MLA_ROPE) ** -0.5
GQA_SCALE = GQA_DIM ** -0.5
DIFF_SCALE = DIFF_QK ** -0.5

LANES = 128
VMEM_LIMIT = 56 * 1024 * 1024
MAX_SEQ = 4096

PREP_TM = 256
ATTN_TQ = 256
PROJ_TM = 512
FFN_TM = 512
FFN_TF = 1408
COMBINE_TM = 256
DMA_UNROLL = 8

BF16 = jnp.bfloat16
F32 = jnp.float32

_SEG = dict(cq=(0, 256), ckv=(256, 384), kr=(384, 512), krr=(512, 640),
            gq=(640, 1024), gqr=(1024, 1408), gk=(1408, 1536), gkr=(1536, 1664),
            gv=(1664, 1792), dq=(1792, 2048), dqr=(2048, 2304), dk=(2304, 2560),
            dkr=(2560, 2816), dv=(2816, 3072))
PREP_N = 3072
GQA_ORDER = (0, 3, 1, 4, 2, 5)


def _rot_cols(w, half=16):
    k, n = w.shape
    w4 = w.reshape(k, n // (2 * half), 2, half)
    return jnp.stack([-w4[:, :, 1, :], w4[:, :, 0, :]], axis=2).reshape(k, n)


def _rope_tables():
    pos = jnp.arange(MAX_SEQ, dtype=F32)
    inv = ROPE_THETA ** (-jnp.arange(0, 32, 2, dtype=F32) / 32)

    def cs(p):
        ang = p[:, None] * inv[None, :]
        c, s = jnp.cos(ang), jnp.sin(ang)
        return jnp.concatenate([c, c], -1), jnp.concatenate([s, s], -1)

    c32, s32 = cs(pos)
    row = jnp.floor(pos / GRID_W)
    col = pos - row * GRID_W
    cr, sr = cs(row)
    cc, sc = cs(col)
    ones = jnp.ones((MAX_SEQ, 64), F32)
    z64 = jnp.zeros((MAX_SEQ, 64), F32)
    z32 = jnp.zeros((MAX_SEQ, 32), F32)
    a_c = jnp.concatenate([ones, c32, z32], -1)
    a_s = jnp.concatenate([z64, s32, z32], -1)
    p_c = jnp.tile(c32, (1, 4))
    p_s = jnp.tile(s32, (1, 4))
    x_c = jnp.tile(jnp.concatenate([cr, cc], -1), (1, 2))
    x_s = jnp.tile(jnp.concatenate([sr, sc], -1), (1, 2))
    return a_c, a_s, p_c, p_s, x_c, x_s


def _layer_weights(l, w_in, mla_q_norm, mla_w_qb, mla_kv_norm, mla_w_kvb, gqa_q_norm,
                   gqa_k_norm, w_out):
    w = w_in[l]
    o = [0, 256, 384, 416, 800, 928, 1056, 1312, 1568, 1824]
    cq, ckv, kr, gq, gk, gv, dq, dk, dv = [w[:, o[i]:o[i + 1]] for i in range(9)]
    z = lambda n: jnp.zeros((D_MODEL, n), F32)
    kr_p = jnp.concatenate([z(64), kr, z(32)], -1)
    krr_p = jnp.concatenate([z(64), _rot_cols(kr), z(32)], -1)
    order = jnp.asarray(GQA_ORDER)
    gq_p = gq.reshape(D_MODEL, GQA_HEADS, GQA_DIM)[:, order, :].reshape(D_MODEL, -1)
    w_all = jnp.concatenate(
        [cq, ckv, kr_p, krr_p, gq_p, _rot_cols(gq_p), gk, _rot_cols(gk), gv,
         dq, _rot_cols(dq), dk, _rot_cols(dk), dv], axis=-1).astype(BF16)

    qb = mla_w_qb[l].reshape(MLA_Q_RANK, MLA_HEADS, MLA_NOPE + MLA_ROPE)
    qb_nope, qb_rope = qb[:, :, :MLA_NOPE], qb[:, :, MLA_NOPE:]
    qb_rope_rot = _rot_cols(qb_rope.reshape(MLA_Q_RANK, -1)).reshape(qb_rope.shape)
    zq = lambda n: jnp.zeros((MLA_Q_RANK, MLA_HEADS, n), F32)
    w_qb = jnp.concatenate(
        [jnp.concatenate([qb_nope, qb_rope, zq(32)], -1).reshape(MLA_Q_RANK, -1),
         jnp.concatenate([zq(64), qb_rope_rot, zq(32)], -1).reshape(MLA_Q_RANK, -1)],
        axis=-1).astype(BF16)

    kvb = mla_w_kvb[l].reshape(MLA_KV_RANK, MLA_HEADS, MLA_NOPE + MLA_V)
    w_kn = jnp.concatenate(
        [kvb[:, :, :MLA_NOPE], jnp.zeros((MLA_KV_RANK, MLA_HEADS, 64), F32)],
        -1).reshape(MLA_KV_RANK, -1).astype(BF16)
    w_v = kvb[:, :, MLA_NOPE:].reshape(MLA_KV_RANK, -1).astype(BF16)

    def swap16(g):
        return g.reshape(2, 2, 16)[:, ::-1, :].reshape(64)

    gqn, gkn = gqa_q_norm[l], gqa_k_norm[l]
    vecs = jnp.zeros((8, 256), F32)
    vecs = vecs.at[0, :].set(mla_q_norm[l])
    vecs = vecs.at[1, :128].set(mla_kv_norm[l])
    vecs = vecs.at[2, :128].set(jnp.tile(gqn, 2))
    vecs = vecs.at[3, :128].set(jnp.tile(swap16(gqn), 2))
    vecs = vecs.at[4, :128].set(jnp.tile(gkn, 2))
    vecs = vecs.at[5, :128].set(jnp.tile(swap16(gkn), 2))

    wo = w_out[l]
    wo_m = wo[:384].astype(BF16)
    wo_g = wo[384:768].reshape(GQA_HEADS, GQA_DIM, D_MODEL)[order, :, :]
    wo_g = wo_g.reshape(384, D_MODEL).astype(BF16)
    wo_d = wo[768:].astype(BF16)
    return dict(w_all=w_all, w_qb=w_qb, w_kn=w_kn, w_v=w_v, vecs=vecs,
                wo_m=wo_m, wo_g=wo_g, wo_d=wo_d)


def _prep_kernel(x_ref, ac_ref, as_ref, pc_ref, ps_ref, xc_ref, xs_ref,
                 wall_ref, wqb_ref, wkn_ref, wv_ref, vec_ref,
                 mq_ref, mk_ref, mv_ref, gq_ref, gk_ref, gv_ref, dq_ref, dk_ref, dv_ref):
    tm = x_ref.shape[0]
    xb = x_ref[...].astype(BF16)
    h = jnp.dot(xb, wall_ref[...], preferred_element_type=F32)
    seg = lambda name: h[:, _SEG[name][0]:_SEG[name][1]]
    a_c, a_s = ac_ref[...], as_ref[...]
    p_c, p_s = pc_ref[...], ps_ref[...]
    x_c, x_s = xc_ref[...], xs_ref[...]

    cq = seg('cq')
    cqn = cq * lax.rsqrt(jnp.mean(cq * cq, axis=-1, keepdims=True) + RMS_EPS)
    cqb = (cqn * vec_ref[0:1, :]).astype(BF16)
    q2 = jnp.dot(cqb, wqb_ref[...], preferred_element_type=F32)
    ckv = seg('ckv')
    ckvn = ckv * lax.rsqrt(jnp.mean(ckv * ckv, axis=-1, keepdims=True) + RMS_EPS)
    ckvb = (ckvn * vec_ref[1:2, 0:128]).astype(BF16)
    kn = jnp.dot(ckvb, wkn_ref[...], preferred_element_type=F32)
    kro = seg('kr') * a_c + seg('krr') * a_s
    for hd in range(MLA_HEADS):
        sl = slice(hd * LANES, (hd + 1) * LANES)
        sr = slice(768 + hd * LANES, 768 + (hd + 1) * LANES)
        mq_ref[:, sl] = ((q2[:, sl] * a_c + q2[:, sr] * a_s) * MLA_SCALE).astype(BF16)
        mk_ref[:, sl] = (kn[:, sl] + kro).astype(BF16)
    mv_ref[...] = jnp.dot(ckvb, wv_ref[...], preferred_element_type=F32).astype(BF16)

    lane = lax.broadcasted_iota(jnp.int32, (tm, LANES), 1)
    lo = lane < 64

    def half_rms(z):
        zz = z * z
        s_lo = jnp.sum(jnp.where(lo, zz, 0.0), axis=-1, keepdims=True)
        s_hi = jnp.sum(jnp.where(lo, 0.0, zz), axis=-1, keepdims=True)
        return jnp.where(lo, lax.rsqrt(s_lo * (1.0 / 64) + RMS_EPS),
                         lax.rsqrt(s_hi * (1.0 / 64) + RMS_EPS))

    gq, gqr = seg('gq'), seg('gqr')
    g_q, g_qs = vec_ref[2:3, 0:128], vec_ref[3:4, 0:128]
    for j in range(3):
        sl = slice(j * LANES, (j + 1) * LANES)
        z, zr = gq[:, sl], gqr[:, sl]
        r = half_rms(z)
        gq_ref[:, sl] = (r * ((z * g_q) * x_c + (zr * g_qs) * x_s) * GQA_SCALE).astype(BF16)
    z, zr = seg('gk'), seg('gkr')
    g_k, g_ks = vec_ref[4:5, 0:128], vec_ref[5:6, 0:128]
    gk_ref[...] = (half_rms(z) * ((z * g_k) * x_c + (zr * g_ks) * x_s)).astype(BF16)
    gv_ref[...] = seg('gv').astype(BF16)

    dq, dqr, dk, dkr = seg('dq'), seg('dqr'), seg('dk'), seg('dkr')
    for j in range(2):
        sl = slice(j * LANES, (j + 1) * LANES)
        dq_ref[:, sl] = ((dq[:, sl] * p_c + dqr[:, sl] * p_s) * DIFF_SCALE).astype(BF16)
        dk_ref[:, sl] = (dk[:, sl] * p_c + dkr[:, sl] * p_s).astype(BF16)
    dv_ref[...] = seg('dv').astype(BF16)


def _prep(x, tables, lw, seq):
    t = x.shape[0]
    tm = PREP_TM
    nblk = seq // tm
    row = lambda w: pl.BlockSpec((tm, w), lambda i: (i, 0))
    tab = pl.BlockSpec((tm, LANES), lambda i: (i % nblk, 0))
    full = lambda a: pl.BlockSpec(a.shape, lambda i: (0, 0))
    widths = (768, 768, 384, 384, 128, 128, 256, 256, 256)
    return pl.pallas_call(
        _prep_kernel,
        grid=(t // tm,),
        in_specs=[row(D_MODEL)] + [tab] * 6
        + [full(lw['w_all']), full(lw['w_qb']), full(lw['w_kn']), full(lw['w_v']),
           full(lw['vecs'])],
        out_specs=[row(w) for w in widths],
        out_shape=[jax.ShapeDtypeStruct((t, w), BF16) for w in widths],
        compiler_params=pltpu.CompilerParams(
            dimension_semantics=("parallel",), vmem_limit_bytes=VMEM_LIMIT),
        name="mixer_prep",
    )(x, *tables, lw['w_all'], lw['w_qb'], lw['w_kn'], lw['w_v'], lw['vecs'])


_NT = (((1,), (1,)), ((), ()))


def _softmax_parts(s):
    m = jnp.max(s, axis=-1, keepdims=True)
    e = jnp.exp(s - m)
    return e, jnp.sum(e, axis=-1, keepdims=True)


def _lane_mask(q, keep):
    return jnp.where(keep, q.astype(F32), 0.0).astype(BF16)


def _pair_attn_kernel(q_ref, k_ref, v_ref, o_ref, *, split_qk):
    q, k, v = q_ref[0], k_ref[0], v_ref[0]
    tq = q.shape[0]
    lane = lax.broadcasted_iota(jnp.int32, (tq, LANES), 1)
    lo = lane < 64
    if split_qk:
        qs = (q[:, :LANES], q[:, LANES:])
        ks = (k[:, :LANES], k[:, LANES:])
    else:
        qs = (_lane_mask(q, lo), _lane_mask(q, jnp.logical_not(lo)))
        ks = (k, k)
    outs = []
    for qa, ka in zip(qs, ks):
        s = lax.dot_general(qa, ka, _NT, preferred_element_type=F32)
        e, l = _softmax_parts(s)
        o = jnp.dot(e.astype(BF16), v, preferred_element_type=F32)
        outs.append(o * (1.0 / l))
    o_ref[0] = jnp.where(lo, outs[0], outs[1]).astype(o_ref.dtype)


def _diff_attn_kernel(lam_ref, gain_ref, q_ref, k_ref, v_ref, o_ref, *, lam_init):
    q, k, v = q_ref[0], k_ref[0], v_ref[0]
    tq = q.shape[0]
    lp = lam_ref[...]
    lam = (jnp.exp(jnp.sum(lp[0:1, :] * lp[1:2, :], axis=-1, keepdims=True))
           - jnp.exp(jnp.sum(lp[2:3, :] * lp[3:4, :], axis=-1, keepdims=True)) + lam_init)
    lane = lax.broadcasted_iota(jnp.int32, (tq, LANES), 1)
    lo = lane < 64
    outs = []
    for hd in range(2):
        m1 = (lane >= 64 * hd) & (lane < 64 * hd + 32)
        m2 = (lane >= 64 * hd + 32) & (lane < 64 * hd + 64)
        s1 = lax.dot_general(_lane_mask(q, m1), k, _NT, preferred_element_type=F32)
        e1, l1 = _softmax_parts(s1)
        s2 = lax.dot_general(_lane_mask(q, m2), k, _NT, preferred_element_type=F32)
        e2, l2 = _softmax_parts(s2)
        w = e1 * (1.0 / l1) - e2 * (lam / l2)
        o = jnp.dot(w.astype(BF16), v, preferred_element_type=F32)
        mine = lo if hd == 0 else jnp.logical_not(lo)
        ms = jnp.sum(jnp.where(mine, o * o, 0.0), axis=-1, keepdims=True) * (1.0 / DIFF_V)
        outs.append(o * lax.rsqrt(ms + RMS_EPS))
    o = jnp.where(lo, outs[0], outs[1]) * gain_ref[...] * (1.0 - lam_init)
    o_ref[0] = o.astype(o_ref.dtype)


def _attention(q, k, v, *, mode, lam_params=None, gain=None, lam_init=None):
    b, s, _ = q.shape
    tq = ATTN_TQ
    if mode == 'mla':
        npairs, qw, kw = 3, 2 * LANES, 2 * LANES
        kmap = lambda bi, p, i: (bi, 0, p)
        vmap = lambda bi, p, i: (bi, 0, p)
    elif mode == 'gqa':
        npairs, qw, kw = 3, LANES, LANES
        kmap = lambda bi, p, i: (bi, 0, 0)
        vmap = lambda bi, p, i: (bi, 0, 0)
    else:
        npairs, qw, kw = 2, LANES, LANES
        kmap = lambda bi, p, i: (bi, 0, p)
        vmap = lambda bi, p, i: (bi, 0, p)
    in_specs = [pl.BlockSpec((1, tq, qw), lambda bi, p, i: (bi, i, p)),
                pl.BlockSpec((1, s, kw), kmap),
                pl.BlockSpec((1, s, LANES), vmap)]
    args = [q, k, v]
    if mode == 'diff':
        body = functools.partial(_diff_attn_kernel, lam_init=lam_init)
        in_specs = [pl.BlockSpec(lam_params.shape, lambda bi, p, i: (0, 0)),
                    pl.BlockSpec(gain.shape, lambda bi, p, i: (0, 0))] + in_specs
        args = [lam_params, gain] + args
    else:
        body = functools.partial(_pair_attn_kernel, split_qk=(mode == 'mla'))
    return pl.pallas_call(
        body,
        grid=(b, npairs, s // tq),
        in_specs=in_specs,
        out_specs=pl.BlockSpec((1, tq, LANES), lambda bi, p, i: (bi, i, p)),
        out_shape=jax.ShapeDtypeStruct((b, s, npairs * LANES), BF16),
        compiler_params=pltpu.CompilerParams(
            dimension_semantics=("parallel", "parallel", "parallel"),
            vmem_limit_bytes=VMEM_LIMIT),
        name="attn_" + mode,
    )(*args)


def _layernorm(z, g, b):
    mu = jnp.mean(z, axis=-1, keepdims=True)
    zc = z - mu
    var = jnp.mean(zc * zc, axis=-1, keepdims=True)
    return zc * lax.rsqrt(var + LN_EPS) * g + b


def _router_top2(x, wr_hi, wr_lo):
    x_hi = x.astype(BF16)
    x_lo = (x - x_hi.astype(F32)).astype(BF16)
    logits = (jnp.dot(x_hi, wr_hi, preferred_element_type=F32)
              + jnp.dot(x_lo, wr_hi, preferred_element_type=F32)
              + jnp.dot(x_hi, wr_lo, preferred_element_type=F32))
    lane = lax.broadcasted_iota(jnp.int32, logits.shape, 1)
    neg = -jnp.inf
    logits = jnp.where(lane < N_EXPERTS, logits, neg)
    m1 = jnp.max(logits, axis=-1, keepdims=True)
    i1 = jnp.min(jnp.where(logits == m1, lane, LANES), axis=-1, keepdims=True)
    rest = jnp.where(lane == i1, neg, logits)
    m2 = jnp.max(rest, axis=-1, keepdims=True)
    i2 = jnp.min(jnp.where(rest == m2, lane, LANES), axis=-1, keepdims=True)
    e2 = jnp.exp(m2 - m1)
    w1 = 1.0 / (1.0 + e2)
    return i1, i2, w1, e2 * w1


def _outproj_kernel(*refs, route):
    if route:
        (x_ref, om_ref, og_ref, od_ref, wm_ref, wg_ref, wd_ref, g_ref, b_ref,
         rhi_ref, rlo_ref, o_ref, ri_ref, rw_ref) = refs
    else:
        (x_ref, om_ref, og_ref, od_ref, wm_ref, wg_ref, wd_ref, g_ref, b_ref,
         o_ref) = refs
    y = (jnp.dot(om_ref[...], wm_ref[...], preferred_element_type=F32)
         + jnp.dot(og_ref[...], wg_ref[...], preferred_element_type=F32)
         + jnp.dot(od_ref[...], wd_ref[...], preferred_element_type=F32))
    x1 = _layernorm(ALPHA * x_ref[...] + y, g_ref[...], b_ref[...])
    o_ref[...] = x1
    if route:
        i1, i2, w1, w2 = _router_top2(x1, rhi_ref[...], rlo_ref[...])
        lane = lax.broadcasted_iota(jnp.int32, ri_ref.shape, 1)
        ri_ref[...] = jnp.where(lane == 0, i1, jnp.where(lane == 1, i2, 0))
        rw_ref[...] = jnp.where(lane == 0, w1, jnp.where(lane == 1, w2, 0.0))


def _outproj(x, o_m, o_g, o_d, lw, g, b, router=None):
    t = x.shape[0]
    tm = PROJ_TM
    route = router is not None
    row = lambda w: pl.BlockSpec((tm, w), lambda i: (i, 0))
    full = lambda a: pl.BlockSpec(a.shape, lambda i: (0, 0))
    in_specs = [row(D_MODEL), row(384), row(384), row(256),
                full(lw['wo_m']), full(lw['wo_g']), full(lw['wo_d']), full(g), full(b)]
    args = [x, o_m, o_g, o_d, lw['wo_m'], lw['wo_g'], lw['wo_d'], g, b]
    out_specs = [row(D_MODEL)]
    out_shape = [jax.ShapeDtypeStruct((t, D_MODEL), F32)]
    if route:
        in_specs += [full(router[0]), full(router[1])]
        args += list(router)
        out_specs += [row(LANES), row(LANES)]
        out_shape += [jax.ShapeDtypeStruct((t, LANES), jnp.int32),
                      jax.ShapeDtypeStruct((t, LANES), F32)]
    outs = pl.pallas_call(
        functools.partial(_outproj_kernel, route=route),
        grid=(t // tm,),
        in_specs=in_specs,
        out_specs=out_specs,
        out_shape=out_shape,
        compiler_params=pltpu.CompilerParams(
            dimension_semantics=("parallel",), vmem_limit_bytes=VMEM_LIMIT),
        name="outproj_ln_route" if route else "outproj_ln",
    )(*args)
    return outs if route else outs[0]


def _swiglu_block(xb, wg, wu, wd):
    gt = jnp.dot(xb, wg, preferred_element_type=F32)
    up = jnp.dot(xb, wu, preferred_element_type=F32)
    act = (gt / (1.0 + jnp.exp(-gt)) * up).astype(BF16)
    return jnp.dot(act, wd, preferred_element_type=F32)


def _ffn_kernel(x_ref, wg_ref, wu_ref, wd_ref, g_ref, b_ref, o_ref, xb_sc, acc_sc):
    f = pl.program_id(1)

    @pl.when(f == 0)
    def _():
        xb_sc[...] = x_ref[...].astype(BF16)
        acc_sc[...] = jnp.zeros_like(acc_sc)

    acc_sc[...] += _swiglu_block(xb_sc[...], wg_ref[...], wu_ref[...], wd_ref[...])

    @pl.when(f == pl.num_programs(1) - 1)
    def _():
        o_ref[...] = _layernorm(ALPHA * x_ref[...] + acc_sc[...], g_ref[...], b_ref[...])


def _ffn(x, wg, wu, wd, g, b):
    t = x.shape[0]
    tm, tf = FFN_TM, FFN_TF
    row = pl.BlockSpec((tm, D_MODEL), lambda i, f: (i, 0))
    full = lambda a: pl.BlockSpec(a.shape, lambda i, f: (0, 0))
    return pl.pallas_call(
        _ffn_kernel,
        grid=(t // tm, D_FF // tf),
        in_specs=[row,
                  pl.BlockSpec((D_MODEL, tf), lambda i, f: (0, f)),
                  pl.BlockSpec((D_MODEL, tf), lambda i, f: (0, f)),
                  pl.BlockSpec((tf, D_MODEL), lambda i, f: (f, 0)),
                  full(g), full(b)],
        out_specs=row,
        out_shape=jax.ShapeDtypeStruct((t, D_MODEL), F32),
        scratch_shapes=[pltpu.VMEM((tm, D_MODEL), BF16), pltpu.VMEM((tm, D_MODEL), F32)],
        compiler_params=pltpu.CompilerParams(
            dimension_semantics=("parallel", "arbitrary"), vmem_limit_bytes=VMEM_LIMIT),
        name="ffn_ln",
    )(x, wg, wu, wd, g, b)


def _route_plan(ri, tm):
    ea = ri[:, :2].reshape(-1)
    n_assign = ea.shape[0]
    experts = jnp.arange(N_EXPERTS, dtype=jnp.int32)
    oh = (ea[:, None] == experts[None, :]).astype(jnp.int32)
    csum = jnp.cumsum(oh, axis=0)
    rank = jnp.sum((csum - oh) * oh, axis=1)
    tiles = (csum[-1] + tm - 1) // tm
    tile_end = jnp.cumsum(tiles)
    start = (tile_end - tiles) * tm
    dest = jnp.sum(oh * start[None, :], axis=1) + rank
    n_tiles = n_assign // tm + N_EXPERTS
    tile_ids = jnp.arange(n_tiles, dtype=jnp.int32)
    tile_expert = jnp.sum((tile_ids[:, None] >= tile_end[None, :]).astype(jnp.int32), axis=1)
    last_used = jnp.max(jnp.where(tiles > 0, experts, 0))
    tile_expert = jnp.minimum(tile_expert, last_used).astype(jnp.int32)
    return dict(dest=dest.astype(jnp.int32), tile_expert=tile_expert,
                n_valid=tile_end[-1:].astype(jnp.int32), n_tiles=n_tiles)


def _tile_dest(plan, t, tm):
    return plan['dest'].reshape(t // tm, tm, 2).transpose(0, 2, 1).reshape(t // tm, 1, 2 * tm)


def _row_copy(src, src_row, dst, dst_row, sem):
    return pltpu.make_async_copy(src.at[pl.ds(src_row, 1)], dst.at[pl.ds(dst_row, 1)],
                                 sem.at[0])


def _dispatch_kernel(dest_ref, x_ref, init_hbm, o_hbm, sem):
    del init_hbm
    tm = x_ref.shape[0]

    def issue(j, c):
        _row_copy(x_ref, j, o_hbm, dest_ref[0, 0, j], sem).start()
        _row_copy(x_ref, j, o_hbm, dest_ref[0, 0, tm + j], sem).start()
        return c

    lax.fori_loop(0, tm, issue, 0, unroll=DMA_UNROLL // 2)
    for _ in range(2):
        pltpu.make_async_copy(x_ref, o_hbm.at[pl.ds(0, tm)], sem.at[0]).wait()


def _dispatch(plan, x, row_tile):
    t = x.shape[0]
    tm = COMBINE_TM
    n = plan['n_tiles'] * row_tile
    return pl.pallas_call(
        _dispatch_kernel,
        grid=(t // tm,),
        in_specs=[pl.BlockSpec((1, 1, 2 * tm), lambda i: (i, 0, 0), memory_space=pltpu.SMEM),
                  pl.BlockSpec((tm, D_MODEL), lambda i: (i, 0)),
                  pl.BlockSpec(memory_space=pl.ANY)],
        out_specs=pl.BlockSpec(memory_space=pl.ANY),
        out_shape=jax.ShapeDtypeStruct((n, D_MODEL), x.dtype),
        input_output_aliases={2: 0},
        scratch_shapes=[pltpu.SemaphoreType.DMA((1,))],
        compiler_params=pltpu.CompilerParams(dimension_semantics=("arbitrary",)),
        name="moe_dispatch",
    )(_tile_dest(plan, t, tm), x, jnp.zeros((n, D_MODEL), x.dtype))


def _grouped_ffn_kernel(te_ref, nv_ref, x_ref, wg_ref, wu_ref, wd_ref, o_ref, xb_sc, acc_sc):
    i, f = pl.program_id(0), pl.program_id(1)
    last = f == pl.num_programs(1) - 1
    valid = i < nv_ref[0]

    @pl.when(valid)
    def _():
        @pl.when(f == 0)
        def _():
            xb_sc[...] = x_ref[...].astype(BF16)
            acc_sc[...] = jnp.zeros_like(acc_sc)

        acc_sc[...] += _swiglu_block(xb_sc[...], wg_ref[0], wu_ref[0], wd_ref[0])

        @pl.when(last)
        def _():
            o_ref[...] = acc_sc[...]

    @pl.when(jnp.logical_and(jnp.logical_not(valid), last))
    def _():
        o_ref[...] = jnp.zeros_like(o_ref)


def _grouped_ffn(plan, xs, wg, wu, wd, tm):
    tf = FFN_TF
    row = pl.BlockSpec((tm, D_MODEL), lambda i, f, te, nv: (i, 0))
    return pl.pallas_call(
        _grouped_ffn_kernel,
        grid_spec=pltpu.PrefetchScalarGridSpec(
            num_scalar_prefetch=2,
            grid=(plan['n_tiles'], D_FF // tf),
            in_specs=[row,
                      pl.BlockSpec((1, D_MODEL, tf), lambda i, f, te, nv: (te[i], 0, f)),
                      pl.BlockSpec((1, D_MODEL, tf), lambda i, f, te, nv: (te[i], 0, f)),
                      pl.BlockSpec((1, tf, D_MODEL), lambda i, f, te, nv: (te[i], f, 0))],
            out_specs=row,
            scratch_shapes=[pltpu.VMEM((tm, D_MODEL), BF16),
                            pltpu.VMEM((tm, D_MODEL), F32)]),
        out_shape=jax.ShapeDtypeStruct(xs.shape, F32),
        compiler_params=pltpu.CompilerParams(
            dimension_semantics=("parallel", "arbitrary"), vmem_limit_bytes=VMEM_LIMIT),
        name="moe_grouped_ffn",
    )(plan['tile_expert'], plan['n_valid'], xs, wg, wu, wd)


def _combine_kernel(dest_ref, x_ref, w_ref, ys_hbm, g_ref, b_ref, o_ref, ybuf, sem):
    tm = x_ref.shape[0]

    def issue(j, c):
        _row_copy(ys_hbm, dest_ref[0, 0, j], ybuf, j, sem).start()
        return c

    lax.fori_loop(0, 2 * tm, issue, 0, unroll=DMA_UNROLL)
    pltpu.make_async_copy(ys_hbm.at[pl.ds(0, 2 * tm)], ybuf, sem.at[0]).wait()
    w = w_ref[...]
    y = w[:, 0:1] * ybuf[0:tm, :] + w[:, 1:2] * ybuf[tm:2 * tm, :]
    o_ref[...] = _layernorm(ALPHA * x_ref[...] + y, g_ref[...], b_ref[...])


def _combine(plan, x, rw, ys, g, b):
    t = x.shape[0]
    tm = COMBINE_TM
    dest = _tile_dest(plan, t, tm)
    full = lambda a: pl.BlockSpec(a.shape, lambda i: (0, 0))
    return pl.pallas_call(
        _combine_kernel,
        grid=(t // tm,),
        in_specs=[pl.BlockSpec((1, 1, 2 * tm), lambda i: (i, 0, 0), memory_space=pltpu.SMEM),
                  pl.BlockSpec((tm, D_MODEL), lambda i: (i, 0)),
                  pl.BlockSpec((tm, LANES), lambda i: (i, 0)),
                  pl.BlockSpec(memory_space=pl.ANY), full(g), full(b)],
        out_specs=pl.BlockSpec((tm, D_MODEL), lambda i: (i, 0)),
        out_shape=jax.ShapeDtypeStruct((t, D_MODEL), F32),
        scratch_shapes=[pltpu.VMEM((2 * tm, D_MODEL), F32), pltpu.SemaphoreType.DMA((1,))],
        compiler_params=pltpu.CompilerParams(dimension_semantics=("arbitrary",)),
        name="moe_combine_ln",
    )(dest, x, rw, ys, g, b)


def kernel(x_prompt, x_sample, w_in, mla_q_norm, mla_w_qb, mla_kv_norm, mla_w_kvb,
           gqa_q_norm, gqa_k_norm, diff_lambda, diff_out_norm, w_out, ln1_g, ln1_b,
           ffn_w_gate, ffn_w_up, ffn_w_down, moe_router, moe_w_gate, moe_w_up,
           moe_w_down, ln2_g, ln2_b):
    tables = _rope_tables()
    layers = [_layer_weights(l, w_in, mla_q_norm, mla_w_qb, mla_kv_norm, mla_w_kvb,
                             gqa_q_norm, gqa_k_norm, w_out) for l in range(DEPTH)]
    ffn_w = (ffn_w_gate.astype(BF16), ffn_w_up.astype(BF16), ffn_w_down.astype(BF16))
    moe_w = (moe_w_gate.astype(BF16), moe_w_up.astype(BF16), moe_w_down.astype(BF16))
    routers = []
    for m in range(moe_router.shape[0]):
        wr = jnp.pad(moe_router[m], ((0, 0), (0, LANES - N_EXPERTS)))
        wr_hi = wr.astype(BF16)
        routers.append((wr_hi, (wr - wr_hi.astype(F32)).astype(BF16)))

    def run(x):
        b, s, _ = x.shape
        t = b * s
        xf = x.reshape(t, D_MODEL)
        for l in range(DEPTH):
            lw = layers[l]
            mq, mk, mv, gq, gk, gv, dq, dk, dv = _prep(xf, tables, lw, s)
            r3 = lambda a: a.reshape(b, s, a.shape[-1])
            o_m = _attention(r3(mq), r3(mk), r3(mv), mode='mla')
            o_g = _attention(r3(gq), r3(gk), r3(gv), mode='gqa')
            lam_init = 0.8 - 0.6 * math.exp(-0.3 * l)
            o_d = _attention(r3(dq), r3(dk), r3(dv), mode='diff',
                             lam_params=diff_lambda[l].astype(F32),
                             gain=jnp.tile(diff_out_norm[l], 2).reshape(1, LANES),
                             lam_init=lam_init)
            mix = (o_m.reshape(t, -1), o_g.reshape(t, -1), o_d.reshape(t, -1))
            g1, b1 = ln1_g[l].reshape(1, -1), ln1_b[l].reshape(1, -1)
            g2, b2 = ln2_g[l].reshape(1, -1), ln2_b[l].reshape(1, -1)
            m = l // 2
            if l % 2 == 0:
                x1 = _outproj(xf, *mix, lw, g1, b1)
                xf = _ffn(x1, ffn_w[0][m], ffn_w[1][m], ffn_w[2][m], g2, b2)
            else:
                x1, ri, rw = _outproj(xf, *mix, lw, g1, b1, router=routers[m])
                plan = _route_plan(ri, FFN_TM)
                xs = _dispatch(plan, x1, FFN_TM)
                ys = _grouped_ffn(plan, xs, moe_w[0][m], moe_w[1][m], moe_w[2][m], FFN_TM)
                xf = _combine(plan, x1, rw, ys, g2, b2)
        return xf.reshape(b, s, D_MODEL)

    return (run(x_prompt), run(x_sample))
```

```python
import functools
import math

import jax
import jax.numpy as jnp
from jax import lax
from jax.experimental import pallas as pl
from jax.experimental.pallas import tpu as pltpu

D_MODEL = 1024
DEPTH = 2
GRID_W = 64
ROPE_THETA = 10000.0
MLA_HEADS = 6
MLA_Q_RANK = 256
MLA_KV_RANK = 128
MLA_NOPE = 64
MLA_ROPE = 32
MLA_V = 64
GQA_HEADS = 6
GQA_KV_HEADS = 2
GQA_DIM = 64
DIFF_HEADS = 4
DIFF_QK = 32
DIFF_V = 64
D_FF = 2816
N_EXPERTS = 8
ALPHA = (2 * DEPTH) ** 0.25
LN_EPS = 1e-5
RMS_EPS = 1e-6

LOG2E = math.log2(math.e)
MLA_SCALE = (MLA_NOPE + MLA_ROPE) ** -0.5 * LOG2E
GQA_SCALE = GQA_DIM ** -0.5 * LOG2E
DIFF_SCALE = DIFF_QK ** -0.5 * LOG2E

LANES = 128
VMEM_LIMIT = 56 * 1024 * 1024
MAX_SEQ = 4096

PREP_TM = 256
ATTN_TQ = 256
ATTN_SB = 64
PIPE_LAG = 2
ONES_ROWS = 16
VT_ROWS = LANES + ONES_ROWS
PROJ_TM = 512
FFN_TM = 512
FFN_TF = 1408
COMBINE_TM = 256
DMA_UNROLL = 8

BF16 = jnp.bfloat16
F32 = jnp.float32

_SEG = dict(cq=(0, 256), ckv=(256, 384), kr=(384, 512), krr=(512, 640),
            gq=(640, 1024), gqr=(1024, 1408), gk=(1408, 1536), gkr=(1536, 1664),
            gv=(1664, 1792), dq=(1792, 2048), dqr=(2048, 2304), dk=(2304, 2560),
            dkr=(2560, 2816), dv=(2816, 3072))
PREP_N = 3072
GQA_ORDER = (0, 3, 1, 4, 2, 5)


def _rot_cols(w, half=16):
    k, n = w.shape
    w4 = w.reshape(k, n // (2 * half), 2, half)
    return jnp.stack([-w4[:, :, 1, :], w4[:, :, 0, :]], axis=2).reshape(k, n)


def _rope_tables():
    pos = jnp.arange(MAX_SEQ, dtype=F32)
    inv = ROPE_THETA ** (-jnp.arange(0, 32, 2, dtype=F32) / 32)

    def cs(p):
        ang = p[:, None] * inv[None, :]
        c, s = jnp.cos(ang), jnp.sin(ang)
        return jnp.concatenate([c, c], -1), jnp.concatenate([s, s], -1)

    c32, s32 = cs(pos)
    row = jnp.floor(pos / GRID_W)
    col = pos - row * GRID_W
    cr, sr = cs(row)
    cc, sc = cs(col)
    ones = jnp.ones((MAX_SEQ, 64), F32)
    z64 = jnp.zeros((MAX_SEQ, 64), F32)
    z32 = jnp.zeros((MAX_SEQ, 32), F32)
    a_c = jnp.concatenate([ones, c32, z32], -1)
    a_s = jnp.concatenate([z64, s32, z32], -1)
    p_c = jnp.tile(c32, (1, 4))
    p_s = jnp.tile(s32, (1, 4))
    x_c = jnp.tile(jnp.concatenate([cr, cc], -1), (1, 2))
    x_s = jnp.tile(jnp.concatenate([sr, sc], -1), (1, 2))
    return a_c, a_s, p_c, p_s, x_c, x_s


def _layer_weights(l, w_in, mla_q_norm, mla_w_qb, mla_kv_norm, mla_w_kvb, gqa_q_norm,
                   gqa_k_norm, w_out):
    w = w_in[l]
    o = [0, 256, 384, 416, 800, 928, 1056, 1312, 1568, 1824]
    cq, ckv, kr, gq, gk, gv, dq, dk, dv = [w[:, o[i]:o[i + 1]] for i in range(9)]
    z = lambda n: jnp.zeros((D_MODEL, n), F32)
    kr_p = jnp.concatenate([z(64), kr, z(32)], -1)
    krr_p = jnp.concatenate([z(64), _rot_cols(kr), z(32)], -1)
    order = jnp.asarray(GQA_ORDER)
    gq_p = gq.reshape(D_MODEL, GQA_HEADS, GQA_DIM)[:, order, :].reshape(D_MODEL, -1)
    w_all = jnp.concatenate(
        [cq, ckv, kr_p, krr_p, gq_p, _rot_cols(gq_p), gk, _rot_cols(gk), gv,
         dq, _rot_cols(dq), dk, _rot_cols(dk), dv], axis=-1).astype(BF16)

    qb = mla_w_qb[l].reshape(MLA_Q_RANK, MLA_HEADS, MLA_NOPE + MLA_ROPE)
    qb_nope, qb_rope = qb[:, :, :MLA_NOPE], qb[:, :, MLA_NOPE:]
    qb_rope_rot = _rot_cols(qb_rope.reshape(MLA_Q_RANK, -1)).reshape(qb_rope.shape)
    zq = lambda n: jnp.zeros((MLA_Q_RANK, MLA_HEADS, n), F32)
    w_qb = jnp.concatenate(
        [jnp.concatenate([qb_nope, qb_rope, zq(32)], -1).reshape(MLA_Q_RANK, -1),
         jnp.concatenate([zq(64), qb_rope_rot, zq(32)], -1).reshape(MLA_Q_RANK, -1)],
        axis=-1).astype(BF16)

    kvb = mla_w_kvb[l].reshape(MLA_KV_RANK, MLA_HEADS, MLA_NOPE + MLA_V)
    w_kn = jnp.concatenate(
        [kvb[:, :, :MLA_NOPE], jnp.zeros((MLA_KV_RANK, MLA_HEADS, 64), F32)],
        -1).reshape(MLA_KV_RANK, -1).astype(BF16)
    w_v = kvb[:, :, MLA_NOPE:].reshape(MLA_KV_RANK, -1).astype(BF16)

    def swap16(g):
        return g.reshape(2, 2, 16)[:, ::-1, :].reshape(64)

    gqn, gkn = gqa_q_norm[l], gqa_k_norm[l]
    vecs = jnp.zeros((8, 256), F32)
    vecs = vecs.at[0, :].set(mla_q_norm[l])
    vecs = vecs.at[1, :128].set(mla_kv_norm[l])
    vecs = vecs.at[2, :128].set(jnp.tile(gqn, 2))
    vecs = vecs.at[3, :128].set(jnp.tile(swap16(gqn), 2))
    vecs = vecs.at[4, :128].set(jnp.tile(gkn, 2))
    vecs = vecs.at[5, :128].set(jnp.tile(swap16(gkn), 2))

    wo = w_out[l]
    wo_m = wo[:384].astype(BF16)
    wo_g = wo[384:768].reshape(GQA_HEADS, GQA_DIM, D_MODEL)[order, :, :]
    wo_g = wo_g.reshape(384, D_MODEL).astype(BF16)
    wo_d = wo[768:].astype(BF16)
    return dict(w_all=w_all, w_qb=w_qb, w_kn=w_kn, w_v=w_v, vecs=vecs,
                wo_m=wo_m, wo_g=wo_g, wo_d=wo_d)


def _store_transposed(vt_ref, v):
    tm = v.shape[0]
    for p in range(v.shape[1] // LANES):
        vt_ref[p * VT_ROWS:p * VT_ROWS + LANES, :] = (
            v[:, p * LANES:(p + 1) * LANES].T.astype(BF16))
        vt_ref[p * VT_ROWS + LANES:(p + 1) * VT_ROWS, :] = jnp.ones((ONES_ROWS, tm), BF16)


def _prep_kernel(x_ref, ac_ref, as_ref, pc_ref, ps_ref, xc_ref, xs_ref,
                 wall_ref, wqb_ref, wkn_ref, wv_ref, vec_ref,
                 mq_ref, mk_ref, mvt_ref, gq_ref, gk_ref, gvt_ref, dq_ref, dk_ref, dvt_ref):
    tm = x_ref.shape[0]
    xb = x_ref[...].astype(BF16)
    h = jnp.dot(xb, wall_ref[...], preferred_element_type=F32)
    seg = lambda name: h[:, _SEG[name][0]:_SEG[name][1]]
    a_c, a_s = ac_ref[...], as_ref[...]
    p_c, p_s = pc_ref[...], ps_ref[...]
    x_c, x_s = xc_ref[...], xs_ref[...]

    cq = seg('cq')
    cqn = cq * lax.rsqrt(jnp.mean(cq * cq, axis=-1, keepdims=True) + RMS_EPS)
    cqb = (cqn * vec_ref[0:1, :]).astype(BF16)
    q2 = jnp.dot(cqb, wqb_ref[...], preferred_element_type=F32)
    ckv = seg('ckv')
    ckvn = ckv * lax.rsqrt(jnp.mean(ckv * ckv, axis=-1, keepdims=True) + RMS_EPS)
    ckvb = (ckvn * vec_ref[1:2, 0:128]).astype(BF16)
    kn = jnp.dot(ckvb, wkn_ref[...], preferred_element_type=F32)
    kro = seg('kr') * a_c + seg('krr') * a_s
    for hd in range(MLA_HEADS):
        sl = slice(hd * LANES, (hd + 1) * LANES)
        sr = slice(768 + hd * LANES, 768 + (hd + 1) * LANES)
        mq_ref[:, sl] = ((q2[:, sl] * a_c + q2[:, sr] * a_s) * MLA_SCALE).astype(BF16)
        mk_ref[:, sl] = (kn[:, sl] + kro).astype(BF16)
    _store_transposed(mvt_ref, jnp.dot(ckvb, wv_ref[...], preferred_element_type=F32))

    lane = lax.broadcasted_iota(jnp.int32, (tm, LANES), 1)
    lo = lane < 64

    def half_rms(z):
        zz = z * z
        s_lo = jnp.sum(jnp.where(lo, zz, 0.0), axis=-1, keepdims=True)
        s_hi = jnp.sum(jnp.where(lo, 0.0, zz), axis=-1, keepdims=True)
        return jnp.where(lo, lax.rsqrt(s_lo * (1.0 / 64) + RMS_EPS),
                         lax.rsqrt(s_hi * (1.0 / 64) + RMS_EPS))

    gq, gqr = seg('gq'), seg('gqr')
    g_q, g_qs = vec_ref[2:3, 0:128], vec_ref[3:4, 0:128]
    for j in range(3):
        sl = slice(j * LANES, (j + 1) * LANES)
        z, zr = gq[:, sl], gqr[:, sl]
        r = half_rms(z)
        gq_ref[:, sl] = (r * ((z * g_q) * x_c + (zr * g_qs) * x_s) * GQA_SCALE).astype(BF16)
    z, zr = seg('gk'), seg('gkr')
    g_k, g_ks = vec_ref[4:5, 0:128], vec_ref[5:6, 0:128]
    gk_ref[...] = (half_rms(z) * ((z * g_k) * x_c + (zr * g_ks) * x_s)).astype(BF16)
    _store_transposed(gvt_ref, seg('gv'))

    dq, dqr, dk, dkr = seg('dq'), seg('dqr'), seg('dk'), seg('dkr')
    for j in range(2):
        sl = slice(j * LANES, (j + 1) * LANES)
        dq_ref[:, sl] = ((dq[:, sl] * p_c + dqr[:, sl] * p_s) * DIFF_SCALE).astype(BF16)
        dk_ref[:, sl] = (dk[:, sl] * p_c + dkr[:, sl] * p_s).astype(BF16)
    _store_transposed(dvt_ref, seg('dv'))


def _prep(x, tables, lw, seq):
    t = x.shape[0]
    tm = PREP_TM
    nblk = seq // tm
    row = lambda w: pl.BlockSpec((tm, w), lambda i: (i, 0))
    tab = pl.BlockSpec((tm, LANES), lambda i: (i % nblk, 0))
    full = lambda a: pl.BlockSpec(a.shape, lambda i: (0, 0))
    col = lambda pairs: pl.BlockSpec((pairs * VT_ROWS, tm), lambda i: (0, i))
    outs = ((768, row), (768, row), (3, col), (384, row), (128, row), (1, col),
            (256, row), (256, row), (2, col))
    return pl.pallas_call(
        _prep_kernel,
        grid=(t // tm,),
        in_specs=[row(D_MODEL)] + [tab] * 6
        + [full(lw['w_all']), full(lw['w_qb']), full(lw['w_kn']), full(lw['w_v']),
           full(lw['vecs'])],
        out_specs=[spec(w) for w, spec in outs],
        out_shape=[jax.ShapeDtypeStruct((w * VT_ROWS, t) if spec is col else (t, w), BF16)
                   for w, spec in outs],
        compiler_params=pltpu.CompilerParams(
            dimension_semantics=("parallel",), vmem_limit_bytes=VMEM_LIMIT),
        name="mixer_prep",
    )(x, *tables, lw['w_all'], lw['w_qb'], lw['w_kn'], lw['w_v'], lw['vecs'])


_NT = (((1,), (1,)), ((), ()))


def _numerators(st):
    m = jnp.max(st, axis=0, keepdims=True)
    return jnp.exp2(st - m).astype(BF16)


def _normalised(oe):
    return oe[:LANES, :] * (1.0 / oe[LANES:LANES + 1, :])


def _lane_mask(q, keep):
    return jnp.where(keep, q.astype(F32), 0.0).astype(BF16)


def _pair_scores(q_ref, k_ref, st, m_sc, split_qk):
    q, k = q_ref[0], k_ref[0]
    tq = q.shape[0]
    if split_qk:
        ops = ((k[:, :LANES], q[:, :LANES]), (k[:, LANES:], q[:, LANES:]))
    else:
        lane = lax.broadcasted_iota(jnp.int32, (tq, LANES), 1)
        ops = ((k, _lane_mask(q, lane < 64)), (k, _lane_mask(q, lane >= 64)))
    for j, (k_op, q_op) in enumerate(ops):
        s_j = lax.dot_general(k_op, q_op, _NT, preferred_element_type=F32)
        st[j][...] = s_j
        m_sc[j:j + 1, :] = jnp.max(s_j, axis=0, keepdims=True)


def _pair_numerators(st, m_sc, e):
    for j in range(2):
        m = m_sc[j:j + 1, :]
        for r in range(st[j].shape[0] // ATTN_SB):
            rows = slice(r * ATTN_SB, (r + 1) * ATTN_SB)
            e[j][rows, :] = jnp.exp2(st[j][rows, :] - m).astype(BF16)


def _pair_pv(vt_ref, e, o_ref):
    vt = vt_ref[...]
    outs = [_normalised(jnp.dot(vt, e[j][...], preferred_element_type=F32))
            for j in range(2)]
    row = lax.broadcasted_iota(jnp.int32, outs[0].shape, 0)
    o_ref[0] = jnp.where(row < 64, outs[0], outs[1]).T.astype(o_ref.dtype)


def _pair_attn_kernel(q_ref, k_ref, vt_ref, o_ref,
                      st00, st01, st10, st11, e00, e01, e10, e11, m0, m1, *, split_qk):
    step = pl.program_id(0)
    st, e, m = ((st00, st01), (st10, st11)), ((e00, e01), (e10, e11)), (m0, m1)

    @pl.when(step == 0)
    def _():
        for buf in (st00, st01, st10, st11, m0, m1):
            buf[...] = jnp.zeros_like(buf)
        for buf in (e00, e01, e10, e11):
            buf[...] = jnp.ones_like(buf)

    for par in range(2):
        @pl.when(step % 2 == par)
        def _():
            _pair_scores(q_ref, k_ref, st[par], m[par], split_qk)
            _pair_numerators(st[1 - par], m[1 - par], e[1 - par])
            _pair_pv(vt_ref, e[par], o_ref)


def _diff_attn_kernel(lam_ref, gain_ref, q_ref, k_ref, vt_ref, o_ref, *, lam_init):
    q, k, vt = q_ref[0], k_ref[0], vt_ref[...]
    tq = q.shape[0]
    lp = lam_ref[...]
    lam = (jnp.exp(jnp.sum(lp[0:1, :] * lp[1:2, :], axis=-1, keepdims=True))
           - jnp.exp(jnp.sum(lp[2:3, :] * lp[3:4, :], axis=-1, keepdims=True)) + lam_init)
    lane = lax.broadcasted_iota(jnp.int32, (tq, LANES), 1)
    qq = jnp.concatenate(
        [_lane_mask(q, (lane >= 32 * j) & (lane < 32 * j + 32)) for j in range(4)], axis=0)
    st = lax.dot_general(k, qq, _NT, preferred_element_type=F32)
    pv = [_normalised(jnp.dot(vt, _numerators(st[:, j * tq:(j + 1) * tq]),
                              preferred_element_type=F32)) for j in range(4)]
    row = lax.broadcasted_iota(jnp.int32, (LANES, tq), 0)

    def head_rms(o, mine):
        ms = jnp.sum(jnp.where(mine, o * o, 0.0), axis=0, keepdims=True) * (1.0 / DIFF_V)
        return o * lax.rsqrt(ms + RMS_EPS)

    ot = jnp.where(row < 64, head_rms(pv[0] - lam * pv[1], row < 64),
                   head_rms(pv[2] - lam * pv[3], row >= 64))
    o_ref[0] = (ot.T * gain_ref[...] * (1.0 - lam_init)).astype(o_ref.dtype)


def _attention(q, k, vt, *, mode, lam_params=None, gain=None, lam_init=None):
    b, s, _ = q.shape
    tq = ATTN_TQ
    if mode == 'mla':
        npairs, qw, kw = 3, 2 * LANES, 2 * LANES
        kmap = lambda bi, p, i: (bi, 0, p)
        vmap = lambda bi, p, i: (p, bi)
    elif mode == 'gqa':
        npairs, qw, kw = 3, LANES, LANES
        kmap = lambda bi, p, i: (bi, 0, 0)
        vmap = lambda bi, p, i: (0, bi)
    else:
        npairs, qw, kw = 2, LANES, LANES
        kmap = lambda bi, p, i: (bi, 0, p)
        vmap = lambda bi, p, i: (p, bi)
    qmap = lambda bi, p, i: (bi, i, p)
    out_shape = jax.ShapeDtypeStruct((b, s, npairs * LANES), BF16)
    if mode == 'diff':
        return pl.pallas_call(
            functools.partial(_diff_attn_kernel, lam_init=lam_init),
            grid=(b, npairs, s // tq),
            in_specs=[pl.BlockSpec(lam_params.shape, lambda bi, p, i: (0, 0)),
                      pl.BlockSpec(gain.shape, lambda bi, p, i: (0, 0)),
                      pl.BlockSpec((1, tq, qw), qmap),
                      pl.BlockSpec((1, s, kw), kmap),
                      pl.BlockSpec((VT_ROWS, s), vmap)],
            out_specs=pl.BlockSpec((1, tq, LANES), qmap),
            out_shape=out_shape,
            compiler_params=pltpu.CompilerParams(
                dimension_semantics=("parallel", "parallel", "parallel"),
                vmem_limit_bytes=VMEM_LIMIT),
            name="attn_diff",
        )(lam_params, gain, q, k, vt)

    nq = s // tq
    n_units = b * npairs * nq

    def lagged(index_map, lag):
        def wrapped(step):
            u = jnp.clip(step - lag, 0, n_units - 1)
            return index_map(u // (nq * npairs), (u // nq) % npairs, u % nq)
        return wrapped

    return pl.pallas_call(
        functools.partial(_pair_attn_kernel, split_qk=(mode == 'mla')),
        grid=(n_units + PIPE_LAG,),
        in_specs=[pl.BlockSpec((1, tq, qw), lagged(qmap, 0)),
                  pl.BlockSpec((1, s, kw), lagged(kmap, 0)),
                  pl.BlockSpec((VT_ROWS, s), lagged(vmap, PIPE_LAG))],
        out_specs=pl.BlockSpec((1, tq, LANES), lagged(qmap, PIPE_LAG)),
        out_shape=out_shape,
        scratch_shapes=([pltpu.VMEM((s, tq), F32)] * 4 + [pltpu.VMEM((s, tq), BF16)] * 4
                        + [pltpu.VMEM((8, tq), F32)] * 2),
        compiler_params=pltpu.CompilerParams(
            dimension_semantics=("arbitrary",), vmem_limit_bytes=VMEM_LIMIT),
        name="attn_" + mode,
    )(q, k, vt)


def _layernorm(z, g, b):
    mu = jnp.mean(z, axis=-1, keepdims=True)
    zc = z - mu
    var = jnp.mean(zc * zc, axis=-1, keepdims=True)
    return zc * lax.rsqrt(var + LN_EPS) * g + b


def _router_top2(x, wr_hi, wr_lo):
    x_hi = x.astype(BF16)
    x_lo = (x - x_hi.astype(F32)).astype(BF16)
    logits = (jnp.dot(x_hi, wr_hi, preferred_element_type=F32)
              + jnp.dot(x_lo, wr_hi, preferred_element_type=F32)
              + jnp.dot(x_hi, wr_lo, preferred_element_type=F32))
    lane = lax.broadcasted_iota(jnp.int32, logits.shape, 1)
    neg = -jnp.inf
    logits = jnp.where(lane < N_EXPERTS, logits, neg)
    m1 = jnp.max(logits, axis=-1, keepdims=True)
    i1 = jnp.min(jnp.where(logits == m1, lane, LANES), axis=-1, keepdims=True)
    rest = jnp.where(lane == i1, neg, logits)
    m2 = jnp.max(rest, axis=-1, keepdims=True)
    i2 = jnp.min(jnp.where(rest == m2, lane, LANES), axis=-1, keepdims=True)
    e2 = jnp.exp(m2 - m1)
    w1 = 1.0 / (1.0 + e2)
    return i1, i2, w1, e2 * w1


def _outproj_kernel(*refs, route):
    if route:
        (x_ref, om_ref, og_ref, od_ref, wm_ref, wg_ref, wd_ref, g_ref, b_ref,
         rhi_ref, rlo_ref, o_ref, ri_ref, rw_ref) = refs
    else:
        (x_ref, om_ref, og_ref, od_ref, wm_ref, wg_ref, wd_ref, g_ref, b_ref,
         o_ref) = refs
    y = (jnp.dot(om_ref[...], wm_ref[...], preferred_element_type=F32)
         + jnp.dot(og_ref[...], wg_ref[...], preferred_element_type=F32)
         + jnp.dot(od_ref[...], wd_ref[...], preferred_element_type=F32))
    x1 = _layernorm(ALPHA * x_ref[...] + y, g_ref[...], b_ref[...])
    o_ref[...] = x1
    if route:
        i1, i2, w1, w2 = _router_top2(x1, rhi_ref[...], rlo_ref[...])
        lane = lax.broadcasted_iota(jnp.int32, ri_ref.shape, 1)
        ri_ref[...] = jnp.where(lane == 0, i1, jnp.where(lane == 1, i2, 0))
        rw_ref[...] = jnp.where(lane == 0, w1, jnp.where(lane == 1, w2, 0.0))


def _outproj(x, o_m, o_g, o_d, lw, g, b, router=None):
    t = x.shape[0]
    tm = PROJ_TM
    route = router is not None
    row = lambda w: pl.BlockSpec((tm, w), lambda i: (i, 0))
    full = lambda a: pl.BlockSpec(a.shape, lambda i: (0, 0))
    in_specs = [row(D_MODEL), row(384), row(384), row(256),
                full(lw['wo_m']), full(lw['wo_g']), full(lw['wo_d']), full(g), full(b)]
    args = [x, o_m, o_g, o_d, lw['wo_m'], lw['wo_g'], lw['wo_d'], g, b]
    out_specs = [row(D_MODEL)]
    out_shape = [jax.ShapeDtypeStruct((t, D_MODEL), F32)]
    if route:
        in_specs += [full(router[0]), full(router[1])]
        args += list(router)
        out_specs += [row(LANES), row(LANES)]
        out_shape += [jax.ShapeDtypeStruct((t, LANES), jnp.int32),
                      jax.ShapeDtypeStruct((t, LANES), F32)]
    outs = pl.pallas_call(
        functools.partial(_outproj_kernel, route=route),
        grid=(t // tm,),
        in_specs=in_specs,
        out_specs=out_specs,
        out_shape=out_shape,
        compiler_params=pltpu.CompilerParams(
            dimension_semantics=("parallel",), vmem_limit_bytes=VMEM_LIMIT),
        name="outproj_ln_route" if route else "outproj_ln",
    )(*args)
    return outs if route else outs[0]


def _swiglu_block(xb, wg, wu, wd):
    gt = jnp.dot(xb, wg, preferred_element_type=F32)
    up = jnp.dot(xb, wu, preferred_element_type=F32)
    act = (gt / (1.0 + jnp.exp(-gt)) * up).astype(BF16)
    return jnp.dot(act, wd, preferred_element_type=F32)


def _ffn_kernel(x_ref, wg_ref, wu_ref, wd_ref, g_ref, b_ref, o_ref, xb_sc, acc_sc):
    f = pl.program_id(1)

    @pl.when(f == 0)
    def _():
        xb_sc[...] = x_ref[...].astype(BF16)
        acc_sc[...] = jnp.zeros_like(acc_sc)

    acc_sc[...] += _swiglu_block(xb_sc[...], wg_ref[...], wu_ref[...], wd_ref[...])

    @pl.when(f == pl.num_programs(1) - 1)
    def _():
        o_ref[...] = _layernorm(ALPHA * x_ref[...] + acc_sc[...], g_ref[...], b_ref[...])


def _ffn(x, wg, wu, wd, g, b):
    t = x.shape[0]
    tm, tf = FFN_TM, FFN_TF
    row = pl.BlockSpec((tm, D_MODEL), lambda i, f: (i, 0))
    full = lambda a: pl.BlockSpec(a.shape, lambda i, f: (0, 0))
    return pl.pallas_call(
        _ffn_kernel,
        grid=(t // tm, D_FF // tf),
        in_specs=[row,
                  pl.BlockSpec((D_MODEL, tf), lambda i, f: (0, f)),
                  pl.BlockSpec((D_MODEL, tf), lambda i, f: (0, f)),
                  pl.BlockSpec((tf, D_MODEL), lambda i, f: (f, 0)),
                  full(g), full(b)],
        out_specs=row,
        out_shape=jax.ShapeDtypeStruct((t, D_MODEL), F32),
        scratch_shapes=[pltpu.VMEM((tm, D_MODEL), BF16), pltpu.VMEM((tm, D_MODEL), F32)],
        compiler_params=pltpu.CompilerParams(
            dimension_semantics=("parallel", "arbitrary"), vmem_limit_bytes=VMEM_LIMIT),
        name="ffn_ln",
    )(x, wg, wu, wd, g, b)


def _route_plan(ri, tm):
    ea = ri[:, :2].reshape(-1)
    n_assign = ea.shape[0]
    experts = jnp.arange(N_EXPERTS, dtype=jnp.int32)
    oh = (ea[:, None] == experts[None, :]).astype(jnp.int32)
    csum = jnp.cumsum(oh, axis=0)
    rank = jnp.sum((csum - oh) * oh, axis=1)
    tiles = (csum[-1] + tm - 1) // tm
    tile_end = jnp.cumsum(tiles)
    start = (tile_end - tiles) * tm
    dest = jnp.sum(oh * start[None, :], axis=1) + rank
    n_tiles = n_assign // tm + N_EXPERTS
    tile_ids = jnp.arange(n_tiles, dtype=jnp.int32)
    tile_expert = jnp.sum((tile_ids[:, None] >= tile_end[None, :]).astype(jnp.int32), axis=1)
    last_used = jnp.max(jnp.where(tiles > 0, experts, 0))
    tile_expert = jnp.minimum(tile_expert, last_used).astype(jnp.int32)
    return dict(dest=dest.astype(jnp.int32), tile_expert=tile_expert,
                n_valid=tile_end[-1:].astype(jnp.int32), n_tiles=n_tiles)


def _tile_dest(plan, t, tm):
    return plan['dest'].reshape(t // tm, tm, 2).transpose(0, 2, 1).reshape(t // tm, 1, 2 * tm)


def _row_copy(src, src_row, dst, dst_row, sem):
    return pltpu.make_async_copy(src.at[pl.ds(src_row, 1)], dst.at[pl.ds(dst_row, 1)],
                                 sem.at[0])


def _dispatch_kernel(dest_ref, x_ref, init_hbm, o_hbm, sem):
    del init_hbm
    tm = x_ref.shape[0]

    def issue(j, c):
        _row_copy(x_ref, j, o_hbm, dest_ref[0, 0, j], sem).start()
        _row_copy(x_ref, j, o_hbm, dest_ref[0, 0, tm + j], sem).start()
        return c

    lax.fori_loop(0, tm, issue, 0, unroll=DMA_UNROLL // 2)
    for _ in range(2):
        pltpu.make_async_copy(x_ref, o_hbm.at[pl.ds(0, tm)], sem.at[0]).wait()


def _dispatch(plan, x, row_tile):
    t = x.shape[0]
    tm = COMBINE_TM
    n = plan['n_tiles'] * row_tile
    return pl.pallas_call(
        _dispatch_kernel,
        grid=(t // tm,),
        in_specs=[pl.BlockSpec((1, 1, 2 * tm), lambda i: (i, 0, 0), memory_space=pltpu.SMEM),
                  pl.BlockSpec((tm, D_MODEL), lambda i: (i, 0)),
                  pl.BlockSpec(memory_space=pl.ANY)],
        out_specs=pl.BlockSpec(memory_space=pl.ANY),
        out_shape=jax.ShapeDtypeStruct((n, D_MODEL), x.dtype),
        input_output_aliases={2: 0},
        scratch_shapes=[pltpu.SemaphoreType.DMA((1,))],
        compiler_params=pltpu.CompilerParams(dimension_semantics=("arbitrary",)),
        name="moe_dispatch",
    )(_tile_dest(plan, t, tm), x, jnp.zeros((n, D_MODEL), x.dtype))


def _grouped_ffn_kernel(te_ref, nv_ref, x_ref, wg_ref, wu_ref, wd_ref, o_ref, xb_sc, acc_sc):
    i, f = pl.program_id(0), pl.program_id(1)
    last = f == pl.num_programs(1) - 1
    valid = i < nv_ref[0]

    @pl.when(valid)
    def _():
        @pl.when(f == 0)
        def _():
            xb_sc[...] = x_ref[...].astype(BF16)
            acc_sc[...] = jnp.zeros_like(acc_sc)

        acc_sc[...] += _swiglu_block(xb_sc[...], wg_ref[0], wu_ref[0], wd_ref[0])

        @pl.when(last)
        def _():
            o_ref[...] = acc_sc[...]

    @pl.when(jnp.logical_and(jnp.logical_not(valid), last))
    def _():
        o_ref[...] = jnp.zeros_like(o_ref)


def _grouped_ffn(plan, xs, wg, wu, wd, tm):
    tf = FFN_TF
    row = pl.BlockSpec((tm, D_MODEL), lambda i, f, te, nv: (i, 0))
    return pl.pallas_call(
        _grouped_ffn_kernel,
        grid_spec=pltpu.PrefetchScalarGridSpec(
            num_scalar_prefetch=2,
            grid=(plan['n_tiles'], D_FF // tf),
            in_specs=[row,
                      pl.BlockSpec((1, D_MODEL, tf), lambda i, f, te, nv: (te[i], 0, f)),
                      pl.BlockSpec((1, D_MODEL, tf), lambda i, f, te, nv: (te[i], 0, f)),
                      pl.BlockSpec((1, tf, D_MODEL), lambda i, f, te, nv: (te[i], f, 0))],
            out_specs=row,
            scratch_shapes=[pltpu.VMEM((tm, D_MODEL), BF16),
                            pltpu.VMEM((tm, D_MODEL), F32)]),
        out_shape=jax.ShapeDtypeStruct(xs.shape, F32),
        compiler_params=pltpu.CompilerParams(
            dimension_semantics=("parallel", "arbitrary"), vmem_limit_bytes=VMEM_LIMIT),
        name="moe_grouped_ffn",
    )(plan['tile_expert'], plan['n_valid'], xs, wg, wu, wd)


def _combine_kernel(dest_ref, x_ref, w_ref, ys_hbm, g_ref, b_ref, o_ref, ybuf, sem):
    tm = x_ref.shape[0]

    def issue(j, c):
        _row_copy(ys_hbm, dest_ref[0, 0, j], ybuf, j, sem).start()
        return c

    lax.fori_loop(0, 2 * tm, issue, 0, unroll=DMA_UNROLL)
    pltpu.make_async_copy(ys_hbm.at[pl.ds(0, 2 * tm)], ybuf, sem.at[0]).wait()
    w = w_ref[...]
    y = w[:, 0:1] * ybuf[0:tm, :] + w[:, 1:2] * ybuf[tm:2 * tm, :]
    o_ref[...] = _layernorm(ALPHA * x_ref[...] + y, g_ref[...], b_ref[...])


def _combine(plan, x, rw, ys, g, b):
    t = x.shape[0]
    tm = COMBINE_TM
    dest = _tile_dest(plan, t, tm)
    full = lambda a: pl.BlockSpec(a.shape, lambda i: (0, 0))
    return pl.pallas_call(
        _combine_kernel,
        grid=(t // tm,),
        in_specs=[pl.BlockSpec((1, 1, 2 * tm), lambda i: (i, 0, 0), memory_space=pltpu.SMEM),
                  pl.BlockSpec((tm, D_MODEL), lambda i: (i, 0)),
                  pl.BlockSpec((tm, LANES), lambda i: (i, 0)),
                  pl.BlockSpec(memory_space=pl.ANY), full(g), full(b)],
        out_specs=pl.BlockSpec((tm, D_MODEL), lambda i: (i, 0)),
        out_shape=jax.ShapeDtypeStruct((t, D_MODEL), F32),
        scratch_shapes=[pltpu.VMEM((2 * tm, D_MODEL), F32), pltpu.SemaphoreType.DMA((1,))],
        compiler_params=pltpu.CompilerParams(dimension_semantics=("arbitrary",)),
        name="moe_combine_ln",
    )(dest, x, rw, ys, g, b)


def kernel(x_prompt, x_sample, w_in, mla_q_norm, mla_w_qb, mla_kv_norm, mla_w_kvb,
           gqa_q_norm, gqa_k_norm, diff_lambda, diff_out_norm, w_out, ln1_g, ln1_b,
           ffn_w_gate, ffn_w_up, ffn_w_down, moe_router, moe_w_gate, moe_w_up,
           moe_w_down, ln2_g, ln2_b):
    tables = _rope_tables()
    layers = [_layer_weights(l, w_in, mla_q_norm, mla_w_qb, mla_kv_norm, mla_w_kvb,
                             gqa_q_norm, gqa_k_norm, w_out) for l in range(DEPTH)]
    ffn_w = (ffn_w_gate.astype(BF16), ffn_w_up.astype(BF16), ffn_w_down.astype(BF16))
    moe_w = (moe_w_gate.astype(BF16), moe_w_up.astype(BF16), moe_w_down.astype(BF16))
    routers = []
    for m in range(moe_router.shape[0]):
        wr = jnp.pad(moe_router[m], ((0, 0), (0, LANES - N_EXPERTS)))
        wr_hi = wr.astype(BF16)
        routers.append((wr_hi, (wr - wr_hi.astype(F32)).astype(BF16)))

    def run(x):
        b, s, _ = x.shape
        t = b * s
        xf = x.reshape(t, D_MODEL)
        for l in range(DEPTH):
            lw = layers[l]
            mq, mk, mvt, gq, gk, gvt, dq, dk, dvt = _prep(xf, tables, lw, s)
            r3 = lambda a: a.reshape(b, s, a.shape[-1])
            o_m = _attention(r3(mq), r3(mk), mvt, mode='mla')
            o_g = _attention(r3(gq), r3(gk), gvt, mode='gqa')
            lam_init = 0.8 - 0.6 * math.exp(-0.3 * l)
            o_d = _attention(r3(dq), r3(dk), dvt, mode='diff',
                             lam_params=diff_lambda[l].astype(F32),
                             gain=jnp.tile(diff_out_norm[l], 2).reshape(1, LANES),
                             lam_init=lam_init)
            mix = (o_m.reshape(t, -1), o_g.reshape(t, -1), o_d.reshape(t, -1))
            g1, b1 = ln1_g[l].reshape(1, -1), ln1_b[l].reshape(1, -1)
            g2, b2 = ln2_g[l].reshape(1, -1), ln2_b[l].reshape(1, -1)
            m = l // 2
            if l % 2 == 0:
                x1 = _outproj(xf, *mix, lw, g1, b1)
                xf = _ffn(x1, ffn_w[0][m], ffn_w[1][m], ffn_w[2][m], g2, b2)
            else:
                x1, ri, rw = _outproj(xf, *mix, lw, g1, b1, router=routers[m])
                plan = _route_plan(ri, FFN_TM)
                xs = _dispatch(plan, x1, FFN_TM)
                ys = _grouped_ffn(plan, xs, moe_w[0][m], moe_w[1][m], moe_w[2][m], FFN_TM)
                xf = _combine(plan, x1, rw, ys, g2, b2)
        return xf.reshape(b, s, D_MODEL)

    return (run(x_prompt), run(x_sample))
```

```python
import functools
import math

import jax
import jax.numpy as jnp
from jax import lax
from jax.experimental import pallas as pl
from jax.experimental.pallas import tpu as pltpu

D_MODEL = 1024
DEPTH = 2
GRID_W = 64
ROPE_THETA = 10000.0
MLA_HEADS = 6
MLA_Q_RANK = 256
MLA_KV_RANK = 128
MLA_NOPE = 64
MLA_ROPE = 32
MLA_V = 64
GQA_HEADS = 6
GQA_KV_HEADS = 2
GQA_DIM = 64
DIFF_HEADS = 4
DIFF_QK = 32
DIFF_V = 64
D_FF = 2816
N_EXPERTS = 8
ALPHA = (2 * DEPTH) ** 0.25
LN_EPS = 1e-5
RMS_EPS = 1e-6

LOG2E = math.log2(math.e)
MLA_SCALE = (MLA_NOPE + MLA_ROPE) ** -0.5 * LOG2E
GQA_SCALE = GQA_DIM ** -0.5 * LOG2E
DIFF_SCALE = DIFF_QK ** -0.5 * LOG2E

LANES = 128
VMEM_LIMIT = 56 * 1024 * 1024
MAX_SEQ = 4096

PREP_TM = 256
ATTN_SCORE_BYTES = 16 * 1024 * 1024
ATTN_SB_ELEMS = 16 * 1024
ATTN_RB = 1024
PIPE_LAG = 2
ONES_ROWS = 16
VT_ROWS = LANES + ONES_ROWS
PROJ_TM = 512
FFN_TM = 512
FFN_TF = 1408
COMBINE_TM = 256
DMA_UNROLL = 8

BF16 = jnp.bfloat16
F32 = jnp.float32

_SEG = dict(cq=(0, 256), ckv=(256, 384), kr=(384, 512), krr=(512, 640),
            gq=(640, 1024), gqr=(1024, 1408), gk=(1408, 1536), gkr=(1536, 1664),
            gv=(1664, 1792), dq=(1792, 2048), dqr=(2048, 2304), dk=(2304, 2560),
            dkr=(2560, 2816), dv=(2816, 3072))
PREP_N = 3072
GQA_ORDER = (0, 3, 1, 4, 2, 5)


def _rot_cols(w, half=16):
    k, n = w.shape
    w4 = w.reshape(k, n // (2 * half), 2, half)
    return jnp.stack([-w4[:, :, 1, :], w4[:, :, 0, :]], axis=2).reshape(k, n)


def _rope_tables():
    pos = jnp.arange(MAX_SEQ, dtype=F32)
    inv = ROPE_THETA ** (-jnp.arange(0, 32, 2, dtype=F32) / 32)

    def cs(p):
        ang = p[:, None] * inv[None, :]
        c, s = jnp.cos(ang), jnp.sin(ang)
        return jnp.concatenate([c, c], -1), jnp.concatenate([s, s], -1)

    c32, s32 = cs(pos)
    row = jnp.floor(pos / GRID_W)
    col = pos - row * GRID_W
    cr, sr = cs(row)
    cc, sc = cs(col)
    ones = jnp.ones((MAX_SEQ, 64), F32)
    z64 = jnp.zeros((MAX_SEQ, 64), F32)
    z32 = jnp.zeros((MAX_SEQ, 32), F32)
    a_c = jnp.concatenate([ones, c32, z32], -1)
    a_s = jnp.concatenate([z64, s32, z32], -1)
    p_c = jnp.tile(c32, (1, 4))
    p_s = jnp.tile(s32, (1, 4))
    x_c = jnp.tile(jnp.concatenate([cr, cc], -1), (1, 2))
    x_s = jnp.tile(jnp.concatenate([sr, sc], -1), (1, 2))
    return a_c, a_s, p_c, p_s, x_c, x_s


def _layer_weights(l, w_in, mla_q_norm, mla_w_qb, mla_kv_norm, mla_w_kvb, gqa_q_norm,
                   gqa_k_norm, w_out):
    w = w_in[l]
    o = [0, 256, 384, 416, 800, 928, 1056, 1312, 1568, 1824]
    cq, ckv, kr, gq, gk, gv, dq, dk, dv = [w[:, o[i]:o[i + 1]] for i in range(9)]
    z = lambda n: jnp.zeros((D_MODEL, n), F32)
    kr_p = jnp.concatenate([z(64), kr, z(32)], -1)
    krr_p = jnp.concatenate([z(64), _rot_cols(kr), z(32)], -1)
    order = jnp.asarray(GQA_ORDER)
    gq_p = gq.reshape(D_MODEL, GQA_HEADS, GQA_DIM)[:, order, :].reshape(D_MODEL, -1)
    w_all = jnp.concatenate(
        [cq, ckv, kr_p, krr_p, gq_p, _rot_cols(gq_p), gk, _rot_cols(gk), gv,
         dq, _rot_cols(dq), dk, _rot_cols(dk), dv], axis=-1).astype(BF16)

    qb = mla_w_qb[l].reshape(MLA_Q_RANK, MLA_HEADS, MLA_NOPE + MLA_ROPE)
    qb_nope, qb_rope = qb[:, :, :MLA_NOPE], qb[:, :, MLA_NOPE:]
    qb_rope_rot = _rot_cols(qb_rope.reshape(MLA_Q_RANK, -1)).reshape(qb_rope.shape)
    zq = lambda n: jnp.zeros((MLA_Q_RANK, MLA_HEADS, n), F32)
    w_qb = jnp.concatenate(
        [jnp.concatenate([qb_nope, qb_rope, zq(32)], -1).reshape(MLA_Q_RANK, -1),
         jnp.concatenate([zq(64), qb_rope_rot, zq(32)], -1).reshape(MLA_Q_RANK, -1)],
        axis=-1).astype(BF16)

    kvb = mla_w_kvb[l].reshape(MLA_KV_RANK, MLA_HEADS, MLA_NOPE + MLA_V)
    w_kn = jnp.concatenate(
        [kvb[:, :, :MLA_NOPE], jnp.zeros((MLA_KV_RANK, MLA_HEADS, 64), F32)],
        -1).reshape(MLA_KV_RANK, -1).astype(BF16)
    w_v = kvb[:, :, MLA_NOPE:].reshape(MLA_KV_RANK, -1).astype(BF16)

    def swap16(g):
        return g.reshape(2, 2, 16)[:, ::-1, :].reshape(64)

    gqn, gkn = gqa_q_norm[l], gqa_k_norm[l]
    vecs = jnp.zeros((8, 256), F32)
    vecs = vecs.at[0, :].set(mla_q_norm[l])
    vecs = vecs.at[1, :128].set(mla_kv_norm[l])
    vecs = vecs.at[2, :128].set(jnp.tile(gqn, 2))
    vecs = vecs.at[3, :128].set(jnp.tile(swap16(gqn), 2))
    vecs = vecs.at[4, :128].set(jnp.tile(gkn, 2))
    vecs = vecs.at[5, :128].set(jnp.tile(swap16(gkn), 2))

    wo = w_out[l]
    wo_m = wo[:384].astype(BF16)
    wo_g = wo[384:768].reshape(GQA_HEADS, GQA_DIM, D_MODEL)[order, :, :]
    wo_g = wo_g.reshape(384, D_MODEL).astype(BF16)
    wo_d = wo[768:].astype(BF16)
    return dict(w_all=w_all, w_qb=w_qb, w_kn=w_kn, w_v=w_v, vecs=vecs,
                wo_m=wo_m, wo_g=wo_g, wo_d=wo_d)


def _store_transposed(vt_ref, v):
    tm = v.shape[0]
    for p in range(v.shape[1] // LANES):
        vt_ref[p * VT_ROWS:p * VT_ROWS + LANES, :] = (
            v[:, p * LANES:(p + 1) * LANES].T.astype(BF16))
        vt_ref[p * VT_ROWS + LANES:(p + 1) * VT_ROWS, :] = jnp.ones((ONES_ROWS, tm), BF16)


def _prep_kernel(x_ref, ac_ref, as_ref, pc_ref, ps_ref, xc_ref, xs_ref,
                 wall_ref, wqb_ref, wkn_ref, wv_ref, vec_ref,
                 mq_ref, mk_ref, mvt_ref, gq_ref, gk_ref, gvt_ref, dq_ref, dk_ref, dvt_ref):
    tm = x_ref.shape[0]
    xb = x_ref[...].astype(BF16)
    h = jnp.dot(xb, wall_ref[...], preferred_element_type=F32)
    seg = lambda name: h[:, _SEG[name][0]:_SEG[name][1]]
    a_c, a_s = ac_ref[...], as_ref[...]
    p_c, p_s = pc_ref[...], ps_ref[...]
    x_c, x_s = xc_ref[...], xs_ref[...]

    cq = seg('cq')
    cqn = cq * lax.rsqrt(jnp.mean(cq * cq, axis=-1, keepdims=True) + RMS_EPS)
    cqb = (cqn * vec_ref[0:1, :]).astype(BF16)
    q2 = jnp.dot(cqb, wqb_ref[...], preferred_element_type=F32)
    ckv = seg('ckv')
    ckvn = ckv * lax.rsqrt(jnp.mean(ckv * ckv, axis=-1, keepdims=True) + RMS_EPS)
    ckvb = (ckvn * vec_ref[1:2, 0:128]).astype(BF16)
    kn = jnp.dot(ckvb, wkn_ref[...], preferred_element_type=F32)
    kro = seg('kr') * a_c + seg('krr') * a_s
    for hd in range(MLA_HEADS):
        sl = slice(hd * LANES, (hd + 1) * LANES)
        sr = slice(768 + hd * LANES, 768 + (hd + 1) * LANES)
        mq_ref[:, sl] = ((q2[:, sl] * a_c + q2[:, sr] * a_s) * MLA_SCALE).astype(BF16)
        mk_ref[:, sl] = (kn[:, sl] + kro).astype(BF16)
    _store_transposed(mvt_ref, jnp.dot(ckvb, wv_ref[...], preferred_element_type=F32))

    lane = lax.broadcasted_iota(jnp.int32, (tm, LANES), 1)
    lo = lane < 64

    def half_rms(z):
        zz = z * z
        s_lo = jnp.sum(jnp.where(lo, zz, 0.0), axis=-1, keepdims=True)
        s_hi = jnp.sum(jnp.where(lo, 0.0, zz), axis=-1, keepdims=True)
        return jnp.where(lo, lax.rsqrt(s_lo * (1.0 / 64) + RMS_EPS),
                         lax.rsqrt(s_hi * (1.0 / 64) + RMS_EPS))

    gq, gqr = seg('gq'), seg('gqr')
    g_q, g_qs = vec_ref[2:3, 0:128], vec_ref[3:4, 0:128]
    for j in range(3):
        sl = slice(j * LANES, (j + 1) * LANES)
        z, zr = gq[:, sl], gqr[:, sl]
        r = half_rms(z)
        gq_ref[:, sl] = (r * ((z * g_q) * x_c + (zr * g_qs) * x_s) * GQA_SCALE).astype(BF16)
    z, zr = seg('gk'), seg('gkr')
    g_k, g_ks = vec_ref[4:5, 0:128], vec_ref[5:6, 0:128]
    gk_ref[...] = (half_rms(z) * ((z * g_k) * x_c + (zr * g_ks) * x_s)).astype(BF16)
    _store_transposed(gvt_ref, seg('gv'))

    dq, dqr, dk, dkr = seg('dq'), seg('dqr'), seg('dk'), seg('dkr')
    for j in range(2):
        sl = slice(j * LANES, (j + 1) * LANES)
        dq_ref[:, sl] = ((dq[:, sl] * p_c + dqr[:, sl] * p_s) * DIFF_SCALE).astype(BF16)
        dk_ref[:, sl] = (dk[:, sl] * p_c + dkr[:, sl] * p_s).astype(BF16)
    _store_transposed(dvt_ref, seg('dv'))


def _prep(x, tables, lw, seq):
    t = x.shape[0]
    tm = PREP_TM
    nblk = seq // tm
    row = lambda w: pl.BlockSpec((tm, w), lambda i: (i, 0))
    tab = pl.BlockSpec((tm, LANES), lambda i: (i % nblk, 0))
    full = lambda a: pl.BlockSpec(a.shape, lambda i: (0, 0))
    col = lambda pairs: pl.BlockSpec((pairs * VT_ROWS, tm), lambda i: (0, i))
    outs = ((768, row), (768, row), (3, col), (384, row), (128, row), (1, col),
            (256, row), (256, row), (2, col))
    return pl.pallas_call(
        _prep_kernel,
        grid=(t // tm,),
        in_specs=[row(D_MODEL)] + [tab] * 6
        + [full(lw['w_all']), full(lw['w_qb']), full(lw['w_kn']), full(lw['w_v']),
           full(lw['vecs'])],
        out_specs=[spec(w) for w, spec in outs],
        out_shape=[jax.ShapeDtypeStruct((w * VT_ROWS, t) if spec is col else (t, w), BF16)
                   for w, spec in outs],
        compiler_params=pltpu.CompilerParams(
            dimension_semantics=("parallel",), vmem_limit_bytes=VMEM_LIMIT),
        name="mixer_prep",
    )(x, *tables, lw['w_all'], lw['w_qb'], lw['w_kn'], lw['w_v'], lw['vecs'])


_NT = (((1,), (1,)), ((), ()))


def _normalised(oe):
    return oe[:LANES, :] * (1.0 / oe[LANES:LANES + 1, :])


def _lane_mask(q, keep):
    return jnp.where(keep, q.astype(F32), 0.0).astype(BF16)


def _stage_scores_numerators(q_ref, k_ref, st_new, m_new, st_old, m_old, e_old, mode, head):
    q = q_ref[0]
    tq = q.shape[0]
    n_keys = k_ref.shape[1]
    lane = lax.broadcasted_iota(jnp.int32, (tq, LANES), 1)
    lo_l, hi_l = slice(0, LANES), slice(LANES, 2 * LANES)
    if mode == 'mla':
        ops = ((lo_l, q[:, :LANES]), (hi_l, q[:, LANES:]))
    elif mode == 'gqa':
        ops = ((lo_l, _lane_mask(q, lane < 64)), (lo_l, _lane_mask(q, lane >= 64)))
    else:
        lo = 64 * head
        ops = tuple((lo_l, _lane_mask(q, (lane >= lo + 32 * j) & (lane < lo + 32 * j + 32)))
                    for j in range(2))
    sb = ATTN_SB_ELEMS // tq
    for j, (k_lanes, q_op) in enumerate(ops):
        m_run = None
        m_o = m_old[j:j + 1, :]
        for c in range(n_keys // ATTN_RB):
            rows = slice(c * ATTN_RB, (c + 1) * ATTN_RB)
            s_c = lax.dot_general(k_ref[0, rows, k_lanes], q_op, _NT,
                                  preferred_element_type=F32)
            st_new[j][rows, :] = s_c
            cm = jnp.max(s_c, axis=0, keepdims=True)
            m_run = cm if m_run is None else jnp.maximum(m_run, cm)
            for r in range(ATTN_RB // sb):
                rr = slice(c * ATTN_RB + r * sb, c * ATTN_RB + (r + 1) * sb)
                e_old[j][rr, :] = jnp.exp2(st_old[j][rr, :] - m_o).astype(BF16)
        m_new[j:j + 1, :] = m_run


def _stage_pv(vt_ref, e, o_ref, mode, head, lam_ref, gain_ref, lam_init):
    vt = vt_ref[...]
    pv = [_normalised(jnp.dot(vt, e[j][...], preferred_element_type=F32)) for j in range(2)]
    row = lax.broadcasted_iota(jnp.int32, pv[0].shape, 0)
    if mode != 'diff':
        o_ref[0] = jnp.where(row < 64, pv[0], pv[1]).T.astype(o_ref.dtype)
        return
    lp = lam_ref[...]
    lam = (jnp.exp(jnp.sum(lp[0:1, :] * lp[1:2, :], axis=-1, keepdims=True))
           - jnp.exp(jnp.sum(lp[2:3, :] * lp[3:4, :], axis=-1, keepdims=True)) + lam_init)
    mine = (row >= 64 * head) & (row < 64 * head + 64)
    o = jnp.where(mine, pv[0] - lam * pv[1], 0.0)
    ms = jnp.sum(o * o, axis=0, keepdims=True) * (1.0 / DIFF_V)
    new = (o * lax.rsqrt(ms + RMS_EPS)).T * gain_ref[...] * (1.0 - lam_init)
    if head == 0:
        o_ref[0] = new.astype(o_ref.dtype)
    else:
        lane = lax.broadcasted_iota(jnp.int32, new.shape, 1)
        o_ref[0] = jnp.where(lane >= 64, new, o_ref[0].astype(F32)).astype(o_ref.dtype)


def _attn_kernel(*refs, mode, lam_init):
    if mode == 'diff':
        lam_ref, gain_ref, refs = refs[0], refs[1], refs[2:]
    else:
        lam_ref = gain_ref = None
    q_ref, k_ref, vt_ref, o_ref, st00, st01, st10, st11, e00, e01, e10, e11, m0, m1 = refs
    step = pl.program_id(0)
    st, e, m = ((st00, st01), (st10, st11)), ((e00, e01), (e10, e11)), (m0, m1)

    @pl.when(step == 0)
    def _():
        for buf in (st00, st01, st10, st11, m0, m1):
            buf[...] = jnp.zeros_like(buf)
        for buf in (e00, e01, e10, e11):
            buf[...] = jnp.ones_like(buf)

    for par in range(2):
        @pl.when(step % 2 == par)
        def _():
            _stage_scores_numerators(q_ref, k_ref, st[par], m[par], st[1 - par], m[1 - par],
                                     e[1 - par], mode, par)
            _stage_pv(vt_ref, e[par], o_ref, mode, par, lam_ref, gain_ref, lam_init)


def _attention(q, k, vt, *, mode, lam_params=None, gain=None, lam_init=None):
    b, s, _ = q.shape
    tq = ATTN_SCORE_BYTES // (4 * 4 * s)
    if mode == 'mla':
        npairs, qw, kw, per_tile = 3, 2 * LANES, 2 * LANES, 1
        kmap = lambda bi, p, i: (bi, 0, p)
        vmap = lambda bi, p, i: (p, bi)
    elif mode == 'gqa':
        npairs, qw, kw, per_tile = 3, LANES, LANES, 1
        kmap = lambda bi, p, i: (bi, 0, 0)
        vmap = lambda bi, p, i: (0, bi)
    else:
        npairs, qw, kw, per_tile = 2, LANES, LANES, 2
        kmap = lambda bi, p, i: (bi, 0, p)
        vmap = lambda bi, p, i: (p, bi)
    qmap = lambda bi, p, i: (bi, i, p)
    nq = s // tq
    n_units = b * npairs * nq * per_tile

    def lagged(index_map, lag):
        def wrapped(step):
            tile = jnp.clip(step - lag, 0, n_units - 1) // per_tile
            return index_map(tile // (nq * npairs), (tile // nq) % npairs, tile % nq)
        return wrapped

    in_specs = [pl.BlockSpec((1, tq, qw), lagged(qmap, 0)),
                pl.BlockSpec((1, s, kw), lagged(kmap, 0)),
                pl.BlockSpec((VT_ROWS, s), lagged(vmap, PIPE_LAG))]
    args = [q, k, vt]
    if mode == 'diff':
        in_specs = [pl.BlockSpec(lam_params.shape, lambda step: (0, 0)),
                    pl.BlockSpec(gain.shape, lambda step: (0, 0))] + in_specs
        args = [lam_params, gain] + args
    return pl.pallas_call(
        functools.partial(_attn_kernel, mode=mode, lam_init=lam_init),
        grid=(n_units + PIPE_LAG,),
        in_specs=in_specs,
        out_specs=pl.BlockSpec((1, tq, LANES), lagged(qmap, PIPE_LAG)),
        out_shape=jax.ShapeDtypeStruct((b, s, npairs * LANES), BF16),
        scratch_shapes=([pltpu.VMEM((s, tq), F32)] * 4 + [pltpu.VMEM((s, tq), BF16)] * 4
                        + [pltpu.VMEM((8, tq), F32)] * 2),
        compiler_params=pltpu.CompilerParams(
            dimension_semantics=("arbitrary",), vmem_limit_bytes=VMEM_LIMIT),
        name="attn_" + mode,
    )(*args)


def _layernorm(z, g, b):
    mu = jnp.mean(z, axis=-1, keepdims=True)
    zc = z - mu
    var = jnp.mean(zc * zc, axis=-1, keepdims=True)
    return zc * lax.rsqrt(var + LN_EPS) * g + b


def _router_top2(x, wr_hi, wr_lo):
    x_hi = x.astype(BF16)
    x_lo = (x - x_hi.astype(F32)).astype(BF16)
    logits = (jnp.dot(x_hi, wr_hi, preferred_element_type=F32)
              + jnp.dot(x_lo, wr_hi, preferred_element_type=F32)
              + jnp.dot(x_hi, wr_lo, preferred_element_type=F32))
    lane = lax.broadcasted_iota(jnp.int32, logits.shape, 1)
    neg = -jnp.inf
    logits = jnp.where(lane < N_EXPERTS, logits, neg)
    m1 = jnp.max(logits, axis=-1, keepdims=True)
    i1 = jnp.min(jnp.where(logits == m1, lane, LANES), axis=-1, keepdims=True)
    rest = jnp.where(lane == i1, neg, logits)
    m2 = jnp.max(rest, axis=-1, keepdims=True)
    i2 = jnp.min(jnp.where(rest == m2, lane, LANES), axis=-1, keepdims=True)
    e2 = jnp.exp(m2 - m1)
    w1 = 1.0 / (1.0 + e2)
    return i1, i2, w1, e2 * w1


def _outproj_kernel(*refs, route):
    if route:
        (x_ref, om_ref, og_ref, od_ref, wm_ref, wg_ref, wd_ref, g_ref, b_ref,
         rhi_ref, rlo_ref, o_ref, ri_ref, rw_ref) = refs
    else:
        (x_ref, om_ref, og_ref, od_ref, wm_ref, wg_ref, wd_ref, g_ref, b_ref,
         o_ref) = refs
    y = (jnp.dot(om_ref[...], wm_ref[...], preferred_element_type=F32)
         + jnp.dot(og_ref[...], wg_ref[...], preferred_element_type=F32)
         + jnp.dot(od_ref[...], wd_ref[...], preferred_element_type=F32))
    x1 = _layernorm(ALPHA * x_ref[...] + y, g_ref[...], b_ref[...])
    o_ref[...] = x1
    if route:
        i1, i2, w1, w2 = _router_top2(x1, rhi_ref[...], rlo_ref[...])
        lane = lax.broadcasted_iota(jnp.int32, ri_ref.shape, 1)
        ri_ref[...] = jnp.where(lane == 0, i1, jnp.where(lane == 1, i2, 0))
        rw_ref[...] = jnp.where(lane == 0, w1, jnp.where(lane == 1, w2, 0.0))


def _outproj(x, o_m, o_g, o_d, lw, g, b, router=None):
    t = x.shape[0]
    tm = PROJ_TM
    route = router is not None
    row = lambda w: pl.BlockSpec((tm, w), lambda i: (i, 0))
    full = lambda a: pl.BlockSpec(a.shape, lambda i: (0, 0))
    in_specs = [row(D_MODEL), row(384), row(384), row(256),
                full(lw['wo_m']), full(lw['wo_g']), full(lw['wo_d']), full(g), full(b)]
    args = [x, o_m, o_g, o_d, lw['wo_m'], lw['wo_g'], lw['wo_d'], g, b]
    out_specs = [row(D_MODEL)]
    out_shape = [jax.ShapeDtypeStruct((t, D_MODEL), F32)]
    if route:
        in_specs += [full(router[0]), full(router[1])]
        args += list(router)
        out_specs += [row(LANES), row(LANES)]
        out_shape += [jax.ShapeDtypeStruct((t, LANES), jnp.int32),
                      jax.ShapeDtypeStruct((t, LANES), F32)]
    outs = pl.pallas_call(
        functools.partial(_outproj_kernel, route=route),
        grid=(t // tm,),
        in_specs=in_specs,
        out_specs=out_specs,
        out_shape=out_shape,
        compiler_params=pltpu.CompilerParams(
            dimension_semantics=("parallel",), vmem_limit_bytes=VMEM_LIMIT),
        name="outproj_ln_route" if route else "outproj_ln",
    )(*args)
    return outs if route else outs[0]


def _swiglu_block(xb, wg, wu, wd):
    gt = jnp.dot(xb, wg, preferred_element_type=F32)
    up = jnp.dot(xb, wu, preferred_element_type=F32)
    act = (gt / (1.0 + jnp.exp(-gt)) * up).astype(BF16)
    return jnp.dot(act, wd, preferred_element_type=F32)


def _ffn_kernel(x_ref, wg_ref, wu_ref, wd_ref, g_ref, b_ref, o_ref, xb_sc, acc_sc):
    f = pl.program_id(1)

    @pl.when(f == 0)
    def _():
        xb_sc[...] = x_ref[...].astype(BF16)
        acc_sc[...] = jnp.zeros_like(acc_sc)

    acc_sc[...] += _swiglu_block(xb_sc[...], wg_ref[...], wu_ref[...], wd_ref[...])

    @pl.when(f == pl.num_programs(1) - 1)
    def _():
        o_ref[...] = _layernorm(ALPHA * x_ref[...] + acc_sc[...], g_ref[...], b_ref[...])


def _ffn(x, wg, wu, wd, g, b):
    t = x.shape[0]
    tm, tf = FFN_TM, FFN_TF
    row = pl.BlockSpec((tm, D_MODEL), lambda i, f: (i, 0))
    full = lambda a: pl.BlockSpec(a.shape, lambda i, f: (0, 0))
    return pl.pallas_call(
        _ffn_kernel,
        grid=(t // tm, D_FF // tf),
        in_specs=[row,
                  pl.BlockSpec((D_MODEL, tf), lambda i, f: (0, f)),
                  pl.BlockSpec((D_MODEL, tf), lambda i, f: (0, f)),
                  pl.BlockSpec((tf, D_MODEL), lambda i, f: (f, 0)),
                  full(g), full(b)],
        out_specs=row,
        out_shape=jax.ShapeDtypeStruct((t, D_MODEL), F32),
        scratch_shapes=[pltpu.VMEM((tm, D_MODEL), BF16), pltpu.VMEM((tm, D_MODEL), F32)],
        compiler_params=pltpu.CompilerParams(
            dimension_semantics=("parallel", "arbitrary"), vmem_limit_bytes=VMEM_LIMIT),
        name="ffn_ln",
    )(x, wg, wu, wd, g, b)


def _route_plan(ri, tm):
    ea = ri[:, :2].reshape(-1)
    n_assign = ea.shape[0]
    experts = jnp.arange(N_EXPERTS, dtype=jnp.int32)
    oh = (ea[:, None] == experts[None, :]).astype(jnp.int32)
    csum = jnp.cumsum(oh, axis=0)
    rank = jnp.sum((csum - oh) * oh, axis=1)
    tiles = (csum[-1] + tm - 1) // tm
    tile_end = jnp.cumsum(tiles)
    start = (tile_end - tiles) * tm
    dest = jnp.sum(oh * start[None, :], axis=1) + rank
    n_tiles = n_assign // tm + N_EXPERTS
    tile_ids = jnp.arange(n_tiles, dtype=jnp.int32)
    tile_expert = jnp.sum((tile_ids[:, None] >= tile_end[None, :]).astype(jnp.int32), axis=1)
    last_used = jnp.max(jnp.where(tiles > 0, experts, 0))
    tile_expert = jnp.minimum(tile_expert, last_used).astype(jnp.int32)
    return dict(dest=dest.astype(jnp.int32), tile_expert=tile_expert,
                n_valid=tile_end[-1:].astype(jnp.int32), n_tiles=n_tiles)


def _tile_dest(plan, t, tm):
    return plan['dest'].reshape(t // tm, tm, 2).transpose(0, 2, 1).reshape(t // tm, 1, 2 * tm)


def _row_copy(src, src_row, dst, dst_row, sem):
    return pltpu.make_async_copy(src.at[pl.ds(src_row, 1)], dst.at[pl.ds(dst_row, 1)],
                                 sem.at[0])


def _dispatch_kernel(dest_ref, x_ref, init_hbm, o_hbm, sem):
    del init_hbm
    tm = x_ref.shape[0]

    def issue(j, c):
        _row_copy(x_ref, j, o_hbm, dest_ref[0, 0, j], sem).start()
        _row_copy(x_ref, j, o_hbm, dest_ref[0, 0, tm + j], sem).start()
        return c

    lax.fori_loop(0, tm, issue, 0, unroll=DMA_UNROLL // 2)
    for _ in range(2):
        pltpu.make_async_copy(x_ref, o_hbm.at[pl.ds(0, tm)], sem.at[0]).wait()


def _dispatch(plan, x, row_tile):
    t = x.shape[0]
    tm = COMBINE_TM
    n = plan['n_tiles'] * row_tile
    return pl.pallas_call(
        _dispatch_kernel,
        grid=(t // tm,),
        in_specs=[pl.BlockSpec((1, 1, 2 * tm), lambda i: (i, 0, 0), memory_space=pltpu.SMEM),
                  pl.BlockSpec((tm, D_MODEL), lambda i: (i, 0)),
                  pl.BlockSpec(memory_space=pl.ANY)],
        out_specs=pl.BlockSpec(memory_space=pl.ANY),
        out_shape=jax.ShapeDtypeStruct((n, D_MODEL), x.dtype),
        input_output_aliases={2: 0},
        scratch_shapes=[pltpu.SemaphoreType.DMA((1,))],
        compiler_params=pltpu.CompilerParams(dimension_semantics=("arbitrary",)),
        name="moe_dispatch",
    )(_tile_dest(plan, t, tm), x, jnp.zeros((n, D_MODEL), x.dtype))


def _grouped_ffn_kernel(te_ref, nv_ref, x_ref, wg_ref, wu_ref, wd_ref, o_ref, xb_sc, acc_sc):
    i, f = pl.program_id(0), pl.program_id(1)
    last = f == pl.num_programs(1) - 1
    valid = i < nv_ref[0]

    @pl.when(valid)
    def _():
        @pl.when(f == 0)
        def _():
            xb_sc[...] = x_ref[...].astype(BF16)
            acc_sc[...] = jnp.zeros_like(acc_sc)

        acc_sc[...] += _swiglu_block(xb_sc[...], wg_ref[0], wu_ref[0], wd_ref[0])

        @pl.when(last)
        def _():
            o_ref[...] = acc_sc[...]

    @pl.when(jnp.logical_and(jnp.logical_not(valid), last))
    def _():
        o_ref[...] = jnp.zeros_like(o_ref)


def _grouped_ffn(plan, xs, wg, wu, wd, tm):
    tf = FFN_TF
    row = pl.BlockSpec((tm, D_MODEL), lambda i, f, te, nv: (i, 0))
    return pl.pallas_call(
        _grouped_ffn_kernel,
        grid_spec=pltpu.PrefetchScalarGridSpec(
            num_scalar_prefetch=2,
            grid=(plan['n_tiles'], D_FF // tf),
            in_specs=[row,
                      pl.BlockSpec((1, D_MODEL, tf), lambda i, f, te, nv: (te[i], 0, f)),
                      pl.BlockSpec((1, D_MODEL, tf), lambda i, f, te, nv: (te[i], 0, f)),
                      pl.BlockSpec((1, tf, D_MODEL), lambda i, f, te, nv: (te[i], f, 0))],
            out_specs=row,
            scratch_shapes=[pltpu.VMEM((tm, D_MODEL), BF16),
                            pltpu.VMEM((tm, D_MODEL), F32)]),
        out_shape=jax.ShapeDtypeStruct(xs.shape, F32),
        compiler_params=pltpu.CompilerParams(
            dimension_semantics=("parallel", "arbitrary"), vmem_limit_bytes=VMEM_LIMIT),
        name="moe_grouped_ffn",
    )(plan['tile_expert'], plan['n_valid'], xs, wg, wu, wd)


def _combine_kernel(dest_ref, x_ref, w_ref, ys_hbm, g_ref, b_ref, o_ref, ybuf, sem):
    tm = x_ref.shape[0]

    def issue(j, c):
        _row_copy(ys_hbm, dest_ref[0, 0, j], ybuf, j, sem).start()
        return c

    lax.fori_loop(0, 2 * tm, issue, 0, unroll=DMA_UNROLL)
    pltpu.make_async_copy(ys_hbm.at[pl.ds(0, 2 * tm)], ybuf, sem.at[0]).wait()
    w = w_ref[...]
    y = w[:, 0:1] * ybuf[0:tm, :] + w[:, 1:2] * ybuf[tm:2 * tm, :]
    o_ref[...] = _layernorm(ALPHA * x_ref[...] + y, g_ref[...], b_ref[...])


def _combine(plan, x, rw, ys, g, b):
    t = x.shape[0]
    tm = COMBINE_TM
    dest = _tile_dest(plan, t, tm)
    full = lambda a: pl.BlockSpec(a.shape, lambda i: (0, 0))
    return pl.pallas_call(
        _combine_kernel,
        grid=(t // tm,),
        in_specs=[pl.BlockSpec((1, 1, 2 * tm), lambda i: (i, 0, 0), memory_space=pltpu.SMEM),
                  pl.BlockSpec((tm, D_MODEL), lambda i: (i, 0)),
                  pl.BlockSpec((tm, LANES), lambda i: (i, 0)),
                  pl.BlockSpec(memory_space=pl.ANY), full(g), full(b)],
        out_specs=pl.BlockSpec((tm, D_MODEL), lambda i: (i, 0)),
        out_shape=jax.ShapeDtypeStruct((t, D_MODEL), F32),
        scratch_shapes=[pltpu.VMEM((2 * tm, D_MODEL), F32), pltpu.SemaphoreType.DMA((1,))],
        compiler_params=pltpu.CompilerParams(dimension_semantics=("arbitrary",)),
        name="moe_combine_ln",
    )(dest, x, rw, ys, g, b)


def kernel(x_prompt, x_sample, w_in, mla_q_norm, mla_w_qb, mla_kv_norm, mla_w_kvb,
           gqa_q_norm, gqa_k_norm, diff_lambda, diff_out_norm, w_out, ln1_g, ln1_b,
           ffn_w_gate, ffn_w_up, ffn_w_down, moe_router, moe_w_gate, moe_w_up,
           moe_w_down, ln2_g, ln2_b):
    tables = _rope_tables()
    layers = [_layer_weights(l, w_in, mla_q_norm, mla_w_qb, mla_kv_norm, mla_w_kvb,
                             gqa_q_norm, gqa_k_norm, w_out) for l in range(DEPTH)]
    ffn_w = (ffn_w_gate.astype(BF16), ffn_w_up.astype(BF16), ffn_w_down.astype(BF16))
    moe_w = (moe_w_gate.astype(BF16), moe_w_up.astype(BF16), moe_w_down.astype(BF16))
    routers = []
    for m in range(moe_router.shape[0]):
        wr = jnp.pad(moe_router[m], ((0, 0), (0, LANES - N_EXPERTS)))
        wr_hi = wr.astype(BF16)
        routers.append((wr_hi, (wr - wr_hi.astype(F32)).astype(BF16)))

    def run(x):
        b, s, _ = x.shape
        t = b * s
        xf = x.reshape(t, D_MODEL)
        for l in range(DEPTH):
            lw = layers[l]
            mq, mk, mvt, gq, gk, gvt, dq, dk, dvt = _prep(xf, tables, lw, s)
            r3 = lambda a: a.reshape(b, s, a.shape[-1])
            o_m = _attention(r3(mq), r3(mk), mvt, mode='mla')
            o_g = _attention(r3(gq), r3(gk), gvt, mode='gqa')
            lam_init = 0.8 - 0.6 * math.exp(-0.3 * l)
            o_d = _attention(r3(dq), r3(dk), dvt, mode='diff',
                             lam_params=diff_lambda[l].astype(F32),
                             gain=jnp.tile(diff_out_norm[l], 2).reshape(1, LANES),
                             lam_init=lam_init)
            mix = (o_m.reshape(t, -1), o_g.reshape(t, -1), o_d.reshape(t, -1))
            g1, b1 = ln1_g[l].reshape(1, -1), ln1_b[l].reshape(1, -1)
            g2, b2 = ln2_g[l].reshape(1, -1), ln2_b[l].reshape(1, -1)
            m = l // 2
            if l % 2 == 0:
                x1 = _outproj(xf, *mix, lw, g1, b1)
                xf = _ffn(x1, ffn_w[0][m], ffn_w[1][m], ffn_w[2][m], g2, b2)
            else:
                x1, ri, rw = _outproj(xf, *mix, lw, g1, b1, router=routers[m])
                plan = _route_plan(ri, FFN_TM)
                xs = _dispatch(plan, x1, FFN_TM)
                ys = _grouped_ffn(plan, xs, moe_w[0][m], moe_w[1][m], moe_w[2][m], FFN_TM)
                xf = _combine(plan, x1, rw, ys, g2, b2)
        return xf.reshape(b, s, D_MODEL)

    return (run(x_prompt), run(x_sample))
```

```python
import functools
import math

import jax
import jax.numpy as jnp
from jax import lax
from jax.experimental import pallas as pl
from jax.experimental.pallas import tpu as pltpu

D_MODEL = 1024
DEPTH = 2
GRID_W = 64
ROPE_THETA = 10000.0
MLA_HEADS = 6
MLA_Q_RANK = 256
MLA_KV_RANK = 128
MLA_NOPE = 64
MLA_ROPE = 32
MLA_V = 64
GQA_HEADS = 6
GQA_KV_HEADS = 2
GQA_DIM = 64
DIFF_HEADS = 4
DIFF_QK = 32
DIFF_V = 64
D_FF = 2816
N_EXPERTS = 8
ALPHA = (2 * DEPTH) ** 0.25
LN_EPS = 1e-5
RMS_EPS = 1e-6

LOG2E = math.log2(math.e)
MLA_SCALE = (MLA_NOPE + MLA_ROPE) ** -0.5 * LOG2E
GQA_SCALE = GQA_DIM ** -0.5 * LOG2E
DIFF_SCALE = DIFF_QK ** -0.5 * LOG2E

LANES = 128
VMEM_LIMIT = 56 * 1024 * 1024
MAX_SEQ = 4096

PREP_TM = 256
ATTN_SCRATCH_BYTES = 32 * 1024 * 1024
ATTN_SB_ELEMS = 16 * 1024
ATTN_RB = 1024
PIPE_LAG = 2
ONES_ROWS = 16
VT_ROWS = LANES + ONES_ROWS
PROJ_TM = 512
FFN_TM = 512
FFN_TF = 1408
COMBINE_TM = 256
DMA_UNROLL = 8

BF16 = jnp.bfloat16
F32 = jnp.float32

_SEG = dict(cq=(0, 256), ckv=(256, 384), kr=(384, 512), krr=(512, 640),
            gq=(640, 1024), gqr=(1024, 1408), gk=(1408, 1536), gkr=(1536, 1664),
            gv=(1664, 1792), dq=(1792, 2048), dqr=(2048, 2304), dk=(2304, 2560),
            dkr=(2560, 2816), dv=(2816, 3072))
PREP_N = 3072
GQA_ORDER = (0, 3, 1, 4, 2, 5)


def _rot_cols(w, half=16):
    k, n = w.shape
    w4 = w.reshape(k, n // (2 * half), 2, half)
    return jnp.stack([-w4[:, :, 1, :], w4[:, :, 0, :]], axis=2).reshape(k, n)


def _rope_tables():
    pos = jnp.arange(MAX_SEQ, dtype=F32)
    inv = ROPE_THETA ** (-jnp.arange(0, 32, 2, dtype=F32) / 32)

    def cs(p):
        ang = p[:, None] * inv[None, :]
        c, s = jnp.cos(ang), jnp.sin(ang)
        return jnp.concatenate([c, c], -1), jnp.concatenate([s, s], -1)

    c32, s32 = cs(pos)
    row = jnp.floor(pos / GRID_W)
    col = pos - row * GRID_W
    cr, sr = cs(row)
    cc, sc = cs(col)
    ones = jnp.ones((MAX_SEQ, 64), F32)
    z64 = jnp.zeros((MAX_SEQ, 64), F32)
    z32 = jnp.zeros((MAX_SEQ, 32), F32)
    a_c = jnp.concatenate([ones, c32, z32], -1)
    a_s = jnp.concatenate([z64, s32, z32], -1)
    p_c = jnp.tile(c32, (1, 4))
    p_s = jnp.tile(s32, (1, 4))
    x_c = jnp.tile(jnp.concatenate([cr, cc], -1), (1, 2))
    x_s = jnp.tile(jnp.concatenate([sr, sc], -1), (1, 2))
    return a_c, a_s, p_c, p_s, x_c, x_s


def _layer_weights(l, w_in, mla_q_norm, mla_w_qb, mla_kv_norm, mla_w_kvb, gqa_q_norm,
                   gqa_k_norm, w_out):
    w = w_in[l]
    o = [0, 256, 384, 416, 800, 928, 1056, 1312, 1568, 1824]
    cq, ckv, kr, gq, gk, gv, dq, dk, dv = [w[:, o[i]:o[i + 1]] for i in range(9)]
    z = lambda n: jnp.zeros((D_MODEL, n), F32)
    kr_p = jnp.concatenate([z(64), kr, z(32)], -1)
    krr_p = jnp.concatenate([z(64), _rot_cols(kr), z(32)], -1)
    order = jnp.asarray(GQA_ORDER)
    gq_p = gq.reshape(D_MODEL, GQA_HEADS, GQA_DIM)[:, order, :].reshape(D_MODEL, -1)
    w_all = jnp.concatenate(
        [cq, ckv, kr_p, krr_p, gq_p, _rot_cols(gq_p), gk, _rot_cols(gk), gv,
         dq, _rot_cols(dq), dk, _rot_cols(dk), dv], axis=-1).astype(BF16)

    qb = mla_w_qb[l].reshape(MLA_Q_RANK, MLA_HEADS, MLA_NOPE + MLA_ROPE)
    qb_nope, qb_rope = qb[:, :, :MLA_NOPE], qb[:, :, MLA_NOPE:]
    qb_rope_rot = _rot_cols(qb_rope.reshape(MLA_Q_RANK, -1)).reshape(qb_rope.shape)
    zq = lambda n: jnp.zeros((MLA_Q_RANK, MLA_HEADS, n), F32)
    w_qb = jnp.concatenate(
        [jnp.concatenate([qb_nope, qb_rope, zq(32)], -1).reshape(MLA_Q_RANK, -1),
         jnp.concatenate([zq(64), qb_rope_rot, zq(32)], -1).reshape(MLA_Q_RANK, -1)],
        axis=-1).astype(BF16)

    kvb = mla_w_kvb[l].reshape(MLA_KV_RANK, MLA_HEADS, MLA_NOPE + MLA_V)
    w_kn = jnp.concatenate(
        [kvb[:, :, :MLA_NOPE], jnp.zeros((MLA_KV_RANK, MLA_HEADS, 64), F32)],
        -1).reshape(MLA_KV_RANK, -1).astype(BF16)
    w_v = kvb[:, :, MLA_NOPE:].reshape(MLA_KV_RANK, -1).astype(BF16)

    def swap16(g):
        return g.reshape(2, 2, 16)[:, ::-1, :].reshape(64)

    gqn, gkn = gqa_q_norm[l], gqa_k_norm[l]
    vecs = jnp.zeros((8, 256), F32)
    vecs = vecs.at[0, :].set(mla_q_norm[l])
    vecs = vecs.at[1, :128].set(mla_kv_norm[l])
    vecs = vecs.at[2, :128].set(jnp.tile(gqn, 2))
    vecs = vecs.at[3, :128].set(jnp.tile(swap16(gqn), 2))
    vecs = vecs.at[4, :128].set(jnp.tile(gkn, 2))
    vecs = vecs.at[5, :128].set(jnp.tile(swap16(gkn), 2))

    wo = w_out[l]
    wo_m = wo[:384].astype(BF16)
    wo_g = wo[384:768].reshape(GQA_HEADS, GQA_DIM, D_MODEL)[order, :, :]
    wo_g = wo_g.reshape(384, D_MODEL).astype(BF16)
    wo_d = wo[768:].astype(BF16)
    return dict(w_all=w_all, w_qb=w_qb, w_kn=w_kn, w_v=w_v, vecs=vecs,
                wo_m=wo_m, wo_g=wo_g, wo_d=wo_d)


def _store_transposed(vt_ref, v):
    tm = v.shape[0]
    for p in range(v.shape[1] // LANES):
        vt_ref[p * VT_ROWS:p * VT_ROWS + LANES, :] = (
            v[:, p * LANES:(p + 1) * LANES].T.astype(BF16))
        vt_ref[p * VT_ROWS + LANES:(p + 1) * VT_ROWS, :] = jnp.ones((ONES_ROWS, tm), BF16)


def _prep_kernel(x_ref, ac_ref, as_ref, pc_ref, ps_ref, xc_ref, xs_ref,
                 wall_ref, wqb_ref, wkn_ref, wv_ref, vec_ref,
                 mq_ref, mk_ref, mvt_ref, gq_ref, gk_ref, gvt_ref, dq_ref, dk_ref, dvt_ref):
    tm = x_ref.shape[0]
    xb = x_ref[...].astype(BF16)
    h = jnp.dot(xb, wall_ref[...], preferred_element_type=F32)
    seg = lambda name: h[:, _SEG[name][0]:_SEG[name][1]]
    a_c, a_s = ac_ref[...], as_ref[...]
    p_c, p_s = pc_ref[...], ps_ref[...]
    x_c, x_s = xc_ref[...], xs_ref[...]

    cq = seg('cq')
    cqn = cq * lax.rsqrt(jnp.mean(cq * cq, axis=-1, keepdims=True) + RMS_EPS)
    cqb = (cqn * vec_ref[0:1, :]).astype(BF16)
    q2 = jnp.dot(cqb, wqb_ref[...], preferred_element_type=F32)
    ckv = seg('ckv')
    ckvn = ckv * lax.rsqrt(jnp.mean(ckv * ckv, axis=-1, keepdims=True) + RMS_EPS)
    ckvb = (ckvn * vec_ref[1:2, 0:128]).astype(BF16)
    kn = jnp.dot(ckvb, wkn_ref[...], preferred_element_type=F32)
    kro = seg('kr') * a_c + seg('krr') * a_s
    for hd in range(MLA_HEADS):
        sl = slice(hd * LANES, (hd + 1) * LANES)
        sr = slice(768 + hd * LANES, 768 + (hd + 1) * LANES)
        mq_ref[:, sl] = ((q2[:, sl] * a_c + q2[:, sr] * a_s) * MLA_SCALE).astype(BF16)
        mk_ref[:, sl] = (kn[:, sl] + kro).astype(BF16)
    _store_transposed(mvt_ref, jnp.dot(ckvb, wv_ref[...], preferred_element_type=F32))

    lane = lax.broadcasted_iota(jnp.int32, (tm, LANES), 1)
    lo = lane < 64

    def half_rms(z):
        zz = z * z
        s_lo = jnp.sum(jnp.where(lo, zz, 0.0), axis=-1, keepdims=True)
        s_hi = jnp.sum(jnp.where(lo, 0.0, zz), axis=-1, keepdims=True)
        return jnp.where(lo, lax.rsqrt(s_lo * (1.0 / 64) + RMS_EPS),
                         lax.rsqrt(s_hi * (1.0 / 64) + RMS_EPS))

    gq, gqr = seg('gq'), seg('gqr')
    g_q, g_qs = vec_ref[2:3, 0:128], vec_ref[3:4, 0:128]
    for j in range(3):
        sl = slice(j * LANES, (j + 1) * LANES)
        z, zr = gq[:, sl], gqr[:, sl]
        r = half_rms(z)
        gq_ref[:, sl] = (r * ((z * g_q) * x_c + (zr * g_qs) * x_s) * GQA_SCALE).astype(BF16)
    z, zr = seg('gk'), seg('gkr')
    g_k, g_ks = vec_ref[4:5, 0:128], vec_ref[5:6, 0:128]
    gk_ref[...] = (half_rms(z) * ((z * g_k) * x_c + (zr * g_ks) * x_s)).astype(BF16)
    _store_transposed(gvt_ref, seg('gv'))

    dq, dqr, dk, dkr = seg('dq'), seg('dqr'), seg('dk'), seg('dkr')
    for j in range(2):
        sl = slice(j * LANES, (j + 1) * LANES)
        dq_ref[:, sl] = ((dq[:, sl] * p_c + dqr[:, sl] * p_s) * DIFF_SCALE).astype(BF16)
        dk_ref[:, sl] = (dk[:, sl] * p_c + dkr[:, sl] * p_s).astype(BF16)
    _store_transposed(dvt_ref, seg('dv'))


def _prep(x, tables, lw, seq):
    t = x.shape[0]
    tm = PREP_TM
    nblk = seq // tm
    row = lambda w: pl.BlockSpec((tm, w), lambda i: (i, 0))
    tab = pl.BlockSpec((tm, LANES), lambda i: (i % nblk, 0))
    full = lambda a: pl.BlockSpec(a.shape, lambda i: (0, 0))
    col = lambda pairs: pl.BlockSpec((pairs * VT_ROWS, tm), lambda i: (0, i))
    outs = ((768, row), (768, row), (3, col), (384, row), (128, row), (1, col),
            (256, row), (256, row), (2, col))
    return pl.pallas_call(
        _prep_kernel,
        grid=(t // tm,),
        in_specs=[row(D_MODEL)] + [tab] * 6
        + [full(lw['w_all']), full(lw['w_qb']), full(lw['w_kn']), full(lw['w_v']),
           full(lw['vecs'])],
        out_specs=[spec(w) for w, spec in outs],
        out_shape=[jax.ShapeDtypeStruct((w * VT_ROWS, t) if spec is col else (t, w), BF16)
                   for w, spec in outs],
        compiler_params=pltpu.CompilerParams(
            dimension_semantics=("parallel",), vmem_limit_bytes=VMEM_LIMIT),
        name="mixer_prep",
    )(x, *tables, lw['w_all'], lw['w_qb'], lw['w_kn'], lw['w_v'], lw['vecs'])


_NT = (((1,), (1,)), ((), ()))


def _normalised(oe):
    return oe[:LANES, :] * (1.0 / oe[LANES:LANES + 1, :])


def _lane_mask(q, keep):
    return jnp.where(keep, q.astype(F32), 0.0).astype(BF16)


def _stage_scores_numerators(q_ref, k_ref, st, m_sc, e_old, mode, head):
    q = q_ref[0]
    tq = q.shape[0]
    n_keys = k_ref.shape[1]
    lane = lax.broadcasted_iota(jnp.int32, (tq, LANES), 1)
    lo_l, hi_l = slice(0, LANES), slice(LANES, 2 * LANES)
    if mode == 'mla':
        ops = ((lo_l, q[:, :LANES]), (hi_l, q[:, LANES:]))
    elif mode == 'gqa':
        ops = ((lo_l, _lane_mask(q, lane < 64)), (lo_l, _lane_mask(q, lane >= 64)))
    else:
        lo = 64 * head
        ops = tuple((lo_l, _lane_mask(q, (lane >= lo + 32 * j) & (lane < lo + 32 * j + 32)))
                    for j in range(2))
    sb = ATTN_SB_ELEMS // tq
    for j, (k_lanes, q_op) in enumerate(ops):
        m_run = None
        m_o = m_sc[j:j + 1, :]
        for c in range(n_keys // ATTN_RB):
            rows = slice(c * ATTN_RB, (c + 1) * ATTN_RB)
            for r in range(ATTN_RB // sb):
                rr = slice(c * ATTN_RB + r * sb, c * ATTN_RB + (r + 1) * sb)
                e_old[j][rr, :] = jnp.exp2(st[j][rr, :] - m_o).astype(BF16)
            s_c = lax.dot_general(k_ref[0, rows, k_lanes], q_op, _NT,
                                  preferred_element_type=F32)
            st[j][rows, :] = s_c
            cm = jnp.max(s_c, axis=0, keepdims=True)
            m_run = cm if m_run is None else jnp.maximum(m_run, cm)
        m_sc[j:j + 1, :] = m_run


def _stage_pv(vt_ref, e, o_ref, mode, head, lam_ref, gain_ref, lam_init):
    vt = vt_ref[...]
    pv = [_normalised(jnp.dot(vt, e[j][...], preferred_element_type=F32)) for j in range(2)]
    row = lax.broadcasted_iota(jnp.int32, pv[0].shape, 0)
    if mode != 'diff':
        o_ref[0] = jnp.where(row < 64, pv[0], pv[1]).T.astype(o_ref.dtype)
        return
    lp = lam_ref[...]
    lam = (jnp.exp(jnp.sum(lp[0:1, :] * lp[1:2, :], axis=-1, keepdims=True))
           - jnp.exp(jnp.sum(lp[2:3, :] * lp[3:4, :], axis=-1, keepdims=True)) + lam_init)
    mine = (row >= 64 * head) & (row < 64 * head + 64)
    o = jnp.where(mine, pv[0] - lam * pv[1], 0.0)
    ms = jnp.sum(o * o, axis=0, keepdims=True) * (1.0 / DIFF_V)
    new = (o * lax.rsqrt(ms + RMS_EPS)).T * gain_ref[...] * (1.0 - lam_init)
    if head == 0:
        o_ref[0] = new.astype(o_ref.dtype)
    else:
        lane = lax.broadcasted_iota(jnp.int32, new.shape, 1)
        o_ref[0] = jnp.where(lane >= 64, new, o_ref[0].astype(F32)).astype(o_ref.dtype)


def _attn_kernel(*refs, mode, lam_init):
    if mode == 'diff':
        lam_ref, gain_ref, refs = refs[0], refs[1], refs[2:]
    else:
        lam_ref = gain_ref = None
    q_ref, k_ref, vt_ref, o_ref, st0, st1, e00, e01, e10, e11, m_sc = refs
    step = pl.program_id(0)
    st, e = (st0, st1), ((e00, e01), (e10, e11))

    @pl.when(step == 0)
    def _():
        for buf in (st0, st1, m_sc):
            buf[...] = jnp.zeros_like(buf)
        for buf in (e00, e01, e10, e11):
            buf[...] = jnp.ones_like(buf)

    for par in range(2):
        @pl.when(step % 2 == par)
        def _():
            _stage_scores_numerators(q_ref, k_ref, st, m_sc, e[1 - par], mode, par)
            _stage_pv(vt_ref, e[par], o_ref, mode, par, lam_ref, gain_ref, lam_init)


def _attention(q, k, vt, *, mode, lam_params=None, gain=None, lam_init=None):
    b, s, _ = q.shape
    tq = ATTN_SCRATCH_BYTES // ((2 * 4 + 4 * 2) * s)
    if mode == 'mla':
        npairs, qw, kw, per_tile = 3, 2 * LANES, 2 * LANES, 1
        kmap = lambda bi, p, i: (bi, 0, p)
        vmap = lambda bi, p, i: (p, bi)
    elif mode == 'gqa':
        npairs, qw, kw, per_tile = 3, LANES, LANES, 1
        kmap = lambda bi, p, i: (bi, 0, 0)
        vmap = lambda bi, p, i: (0, bi)
    else:
        npairs, qw, kw, per_tile = 2, LANES, LANES, 2
        kmap = lambda bi, p, i: (bi, 0, p)
        vmap = lambda bi, p, i: (p, bi)
    qmap = lambda bi, p, i: (bi, i, p)
    nq = s // tq
    n_units = b * npairs * nq * per_tile

    def lagged(index_map, lag):
        def wrapped(step):
            tile = jnp.clip(step - lag, 0, n_units - 1) // per_tile
            return index_map(tile // (nq * npairs), (tile // nq) % npairs, tile % nq)
        return wrapped

    in_specs = [pl.BlockSpec((1, tq, qw), lagged(qmap, 0)),
                pl.BlockSpec((1, s, kw), lagged(kmap, 0)),
                pl.BlockSpec((VT_ROWS, s), lagged(vmap, PIPE_LAG))]
    args = [q, k, vt]
    if mode == 'diff':
        in_specs = [pl.BlockSpec(lam_params.shape, lambda step: (0, 0)),
                    pl.BlockSpec(gain.shape, lambda step: (0, 0))] + in_specs
        args = [lam_params, gain] + args
    return pl.pallas_call(
        functools.partial(_attn_kernel, mode=mode, lam_init=lam_init),
        grid=(n_units + PIPE_LAG,),
        in_specs=in_specs,
        out_specs=pl.BlockSpec((1, tq, LANES), lagged(qmap, PIPE_LAG)),
        out_shape=jax.ShapeDtypeStruct((b, s, npairs * LANES), BF16),
        scratch_shapes=([pltpu.VMEM((s, tq), F32)] * 2 + [pltpu.VMEM((s, tq), BF16)] * 4
                        + [pltpu.VMEM((8, tq), F32)]),
        compiler_params=pltpu.CompilerParams(
            dimension_semantics=("arbitrary",), vmem_limit_bytes=VMEM_LIMIT),
        name="attn_" + mode,
    )(*args)


def _layernorm(z, g, b):
    mu = jnp.mean(z, axis=-1, keepdims=True)
    zc = z - mu
    var = jnp.mean(zc * zc, axis=-1, keepdims=True)
    return zc * lax.rsqrt(var + LN_EPS) * g + b


def _router_top2(x, wr_hi, wr_lo):
    x_hi = x.astype(BF16)
    x_lo = (x - x_hi.astype(F32)).astype(BF16)
    logits = (jnp.dot(x_hi, wr_hi, preferred_element_type=F32)
              + jnp.dot(x_lo, wr_hi, preferred_element_type=F32)
              + jnp.dot(x_hi, wr_lo, preferred_element_type=F32))
    lane = lax.broadcasted_iota(jnp.int32, logits.shape, 1)
    neg = -jnp.inf
    logits = jnp.where(lane < N_EXPERTS, logits, neg)
    m1 = jnp.max(logits, axis=-1, keepdims=True)
    i1 = jnp.min(jnp.where(logits == m1, lane, LANES), axis=-1, keepdims=True)
    rest = jnp.where(lane == i1, neg, logits)
    m2 = jnp.max(rest, axis=-1, keepdims=True)
    i2 = jnp.min(jnp.where(rest == m2, lane, LANES), axis=-1, keepdims=True)
    e2 = jnp.exp(m2 - m1)
    w1 = 1.0 / (1.0 + e2)
    return i1, i2, w1, e2 * w1


def _outproj_kernel(*refs, route):
    if route:
        (x_ref, om_ref, og_ref, od_ref, wm_ref, wg_ref, wd_ref, g_ref, b_ref,
         rhi_ref, rlo_ref, o_ref, ri_ref, rw_ref) = refs
    else:
        (x_ref, om_ref, og_ref, od_ref, wm_ref, wg_ref, wd_ref, g_ref, b_ref,
         o_ref) = refs
    y = (jnp.dot(om_ref[...], wm_ref[...], preferred_element_type=F32)
         + jnp.dot(og_ref[...], wg_ref[...], preferred_element_type=F32)
         + jnp.dot(od_ref[...], wd_ref[...], preferred_element_type=F32))
    x1 = _layernorm(ALPHA * x_ref[...] + y, g_ref[...], b_ref[...])
    o_ref[...] = x1
    if route:
        i1, i2, w1, w2 = _router_top2(x1, rhi_ref[...], rlo_ref[...])
        lane = lax.broadcasted_iota(jnp.int32, ri_ref.shape, 1)
        ri_ref[...] = jnp.where(lane == 0, i1, jnp.where(lane == 1, i2, 0))
        rw_ref[...] = jnp.where(lane == 0, w1, jnp.where(lane == 1, w2, 0.0))


def _outproj(x, o_m, o_g, o_d, lw, g, b, router=None):
    t = x.shape[0]
    tm = PROJ_TM
    route = router is not None
    row = lambda w: pl.BlockSpec((tm, w), lambda i: (i, 0))
    full = lambda a: pl.BlockSpec(a.shape, lambda i: (0, 0))
    in_specs = [row(D_MODEL), row(384), row(384), row(256),
                full(lw['wo_m']), full(lw['wo_g']), full(lw['wo_d']), full(g), full(b)]
    args = [x, o_m, o_g, o_d, lw['wo_m'], lw['wo_g'], lw['wo_d'], g, b]
    out_specs = [row(D_MODEL)]
    out_shape = [jax.ShapeDtypeStruct((t, D_MODEL), F32)]
    if route:
        in_specs += [full(router[0]), full(router[1])]
        args += list(router)
        out_specs += [row(LANES), row(LANES)]
        out_shape += [jax.ShapeDtypeStruct((t, LANES), jnp.int32),
                      jax.ShapeDtypeStruct((t, LANES), F32)]
    outs = pl.pallas_call(
        functools.partial(_outproj_kernel, route=route),
        grid=(t // tm,),
        in_specs=in_specs,
        out_specs=out_specs,
        out_shape=out_shape,
        compiler_params=pltpu.CompilerParams(
            dimension_semantics=("parallel",), vmem_limit_bytes=VMEM_LIMIT),
        name="outproj_ln_route" if route else "outproj_ln",
    )(*args)
    return outs if route else outs[0]


def _swiglu_block(xb, wg, wu, wd):
    gt = jnp.dot(xb, wg, preferred_element_type=F32)
    up = jnp.dot(xb, wu, preferred_element_type=F32)
    act = (gt / (1.0 + jnp.exp(-gt)) * up).astype(BF16)
    return jnp.dot(act, wd, preferred_element_type=F32)


def _ffn_kernel(x_ref, wg_ref, wu_ref, wd_ref, g_ref, b_ref, o_ref, xb_sc, acc_sc):
    f = pl.program_id(1)

    @pl.when(f == 0)
    def _():
        xb_sc[...] = x_ref[...].astype(BF16)
        acc_sc[...] = jnp.zeros_like(acc_sc)

    acc_sc[...] += _swiglu_block(xb_sc[...], wg_ref[...], wu_ref[...], wd_ref[...])

    @pl.when(f == pl.num_programs(1) - 1)
    def _():
        o_ref[...] = _layernorm(ALPHA * x_ref[...] + acc_sc[...], g_ref[...], b_ref[...])


def _ffn(x, wg, wu, wd, g, b):
    t = x.shape[0]
    tm, tf = FFN_TM, D_FF
    row = pl.BlockSpec((tm, D_MODEL), lambda i, f: (i, 0))
    full = lambda a: pl.BlockSpec(a.shape, lambda i, f: (0, 0))
    once = pl.Buffered(1)
    return pl.pallas_call(
        _ffn_kernel,
        grid=(t // tm, D_FF // tf),
        in_specs=[row,
                  pl.BlockSpec((D_MODEL, tf), lambda i, f: (0, f), pipeline_mode=once),
                  pl.BlockSpec((D_MODEL, tf), lambda i, f: (0, f), pipeline_mode=once),
                  pl.BlockSpec((tf, D_MODEL), lambda i, f: (f, 0), pipeline_mode=once),
                  full(g), full(b)],
        out_specs=row,
        out_shape=jax.ShapeDtypeStruct((t, D_MODEL), F32),
        scratch_shapes=[pltpu.VMEM((tm, D_MODEL), BF16), pltpu.VMEM((tm, D_MODEL), F32)],
        compiler_params=pltpu.CompilerParams(
            dimension_semantics=("parallel", "arbitrary"), vmem_limit_bytes=VMEM_LIMIT),
        name="ffn_ln",
    )(x, wg, wu, wd, g, b)


def _route_plan(ri, tm):
    ea = ri[:, :2].reshape(-1)
    n_assign = ea.shape[0]
    experts = jnp.arange(N_EXPERTS, dtype=jnp.int32)
    oh = (ea[:, None] == experts[None, :]).astype(jnp.int32)
    csum = jnp.cumsum(oh, axis=0)
    rank = jnp.sum((csum - oh) * oh, axis=1)
    tiles = (csum[-1] + tm - 1) // tm
    tile_end = jnp.cumsum(tiles)
    start = (tile_end - tiles) * tm
    dest = jnp.sum(oh * start[None, :], axis=1) + rank
    n_tiles = n_assign // tm + N_EXPERTS
    tile_ids = jnp.arange(n_tiles, dtype=jnp.int32)
    tile_expert = jnp.sum((tile_ids[:, None] >= tile_end[None, :]).astype(jnp.int32), axis=1)
    last_used = jnp.max(jnp.where(tiles > 0, experts, 0))
    tile_expert = jnp.minimum(tile_expert, last_used).astype(jnp.int32)
    return dict(dest=dest.astype(jnp.int32), tile_expert=tile_expert,
                n_valid=tile_end[-1:].astype(jnp.int32), n_tiles=n_tiles)


def _tile_dest(plan, t, tm):
    return plan['dest'].reshape(t // tm, tm, 2).transpose(0, 2, 1).reshape(t // tm, 1, 2 * tm)


def _row_copy(src, src_row, dst, dst_row, sem):
    return pltpu.make_async_copy(src.at[pl.ds(src_row, 1)], dst.at[pl.ds(dst_row, 1)],
                                 sem.at[0])


def _dispatch_kernel(dest_ref, x_ref, init_hbm, o_hbm, sem):
    del init_hbm
    tm = x_ref.shape[0]

    def issue(j, c):
        _row_copy(x_ref, j, o_hbm, dest_ref[0, 0, j], sem).start()
        _row_copy(x_ref, j, o_hbm, dest_ref[0, 0, tm + j], sem).start()
        return c

    lax.fori_loop(0, tm, issue, 0, unroll=DMA_UNROLL // 2)
    for _ in range(2):
        pltpu.make_async_copy(x_ref, o_hbm.at[pl.ds(0, tm)], sem.at[0]).wait()


def _dispatch(plan, x, row_tile):
    t = x.shape[0]
    tm = COMBINE_TM
    n = plan['n_tiles'] * row_tile
    return pl.pallas_call(
        _dispatch_kernel,
        grid=(t // tm,),
        in_specs=[pl.BlockSpec((1, 1, 2 * tm), lambda i: (i, 0, 0), memory_space=pltpu.SMEM),
                  pl.BlockSpec((tm, D_MODEL), lambda i: (i, 0)),
                  pl.BlockSpec(memory_space=pl.ANY)],
        out_specs=pl.BlockSpec(memory_space=pl.ANY),
        out_shape=jax.ShapeDtypeStruct((n, D_MODEL), x.dtype),
        input_output_aliases={2: 0},
        scratch_shapes=[pltpu.SemaphoreType.DMA((1,))],
        compiler_params=pltpu.CompilerParams(dimension_semantics=("arbitrary",)),
        name="moe_dispatch",
    )(_tile_dest(plan, t, tm), x, jnp.zeros((n, D_MODEL), x.dtype))


def _grouped_ffn_kernel(te_ref, nv_ref, x_ref, wg_ref, wu_ref, wd_ref, o_ref, xb_sc, acc_sc):
    i, f = pl.program_id(0), pl.program_id(1)
    last = f == pl.num_programs(1) - 1
    valid = i < nv_ref[0]

    @pl.when(valid)
    def _():
        @pl.when(f == 0)
        def _():
            xb_sc[...] = x_ref[...].astype(BF16)
            acc_sc[...] = jnp.zeros_like(acc_sc)

        acc_sc[...] += _swiglu_block(xb_sc[...], wg_ref[0], wu_ref[0], wd_ref[0])

        @pl.when(last)
        def _():
            o_ref[...] = acc_sc[...]

    @pl.when(jnp.logical_and(jnp.logical_not(valid), last))
    def _():
        o_ref[...] = jnp.zeros_like(o_ref)


def _grouped_ffn(plan, xs, wg, wu, wd, tm):
    tf = FFN_TF
    row = pl.BlockSpec((tm, D_MODEL), lambda i, f, te, nv: (i, 0))
    return pl.pallas_call(
        _grouped_ffn_kernel,
        grid_spec=pltpu.PrefetchScalarGridSpec(
            num_scalar_prefetch=2,
            grid=(plan['n_tiles'], D_FF // tf),
            in_specs=[row,
                      pl.BlockSpec((1, D_MODEL, tf), lambda i, f, te, nv: (te[i], 0, f)),
                      pl.BlockSpec((1, D_MODEL, tf), lambda i, f, te, nv: (te[i], 0, f)),
                      pl.BlockSpec((1, tf, D_MODEL), lambda i, f, te, nv: (te[i], f, 0))],
            out_specs=row,
            scratch_shapes=[pltpu.VMEM((tm, D_MODEL), BF16),
                            pltpu.VMEM((tm, D_MODEL), F32)]),
        out_shape=jax.ShapeDtypeStruct(xs.shape, F32),
        compiler_params=pltpu.CompilerParams(
            dimension_semantics=("parallel", "arbitrary"), vmem_limit_bytes=VMEM_LIMIT),
        name="moe_grouped_ffn",
    )(plan['tile_expert'], plan['n_valid'], xs, wg, wu, wd)


def _combine_kernel(dest_ref, x_ref, w_ref, ys_hbm, g_ref, b_ref, o_ref, ybuf, sem):
    tm = x_ref.shape[0]

    def issue(j, c):
        _row_copy(ys_hbm, dest_ref[0, 0, j], ybuf, j, sem).start()
        return c

    lax.fori_loop(0, 2 * tm, issue, 0, unroll=DMA_UNROLL)
    pltpu.make_async_copy(ys_hbm.at[pl.ds(0, 2 * tm)], ybuf, sem.at[0]).wait()
    w = w_ref[...]
    y = w[:, 0:1] * ybuf[0:tm, :] + w[:, 1:2] * ybuf[tm:2 * tm, :]
    o_ref[...] = _layernorm(ALPHA * x_ref[...] + y, g_ref[...], b_ref[...])


def _combine(plan, x, rw, ys, g, b):
    t = x.shape[0]
    tm = COMBINE_TM
    dest = _tile_dest(plan, t, tm)
    full = lambda a: pl.BlockSpec(a.shape, lambda i: (0, 0))
    return pl.pallas_call(
        _combine_kernel,
        grid=(t // tm,),
        in_specs=[pl.BlockSpec((1, 1, 2 * tm), lambda i: (i, 0, 0), memory_space=pltpu.SMEM),
                  pl.BlockSpec((tm, D_MODEL), lambda i: (i, 0)),
                  pl.BlockSpec((tm, LANES), lambda i: (i, 0)),
                  pl.BlockSpec(memory_space=pl.ANY), full(g), full(b)],
        out_specs=pl.BlockSpec((tm, D_MODEL), lambda i: (i, 0)),
        out_shape=jax.ShapeDtypeStruct((t, D_MODEL), F32),
        scratch_shapes=[pltpu.VMEM((2 * tm, D_MODEL), F32), pltpu.SemaphoreType.DMA((1,))],
        compiler_params=pltpu.CompilerParams(dimension_semantics=("arbitrary",)),
        name="moe_combine_ln",
    )(dest, x, rw, ys, g, b)


def kernel(x_prompt, x_sample, w_in, mla_q_norm, mla_w_qb, mla_kv_norm, mla_w_kvb,
           gqa_q_norm, gqa_k_norm, diff_lambda, diff_out_norm, w_out, ln1_g, ln1_b,
           ffn_w_gate, ffn_w_up, ffn_w_down, moe_router, moe_w_gate, moe_w_up,
           moe_w_down, ln2_g, ln2_b):
    tables = _rope_tables()
    layers = [_layer_weights(l, w_in, mla_q_norm, mla_w_qb, mla_kv_norm, mla_w_kvb,
                             gqa_q_norm, gqa_k_norm, w_out) for l in range(DEPTH)]
    ffn_w = (ffn_w_gate.astype(BF16), ffn_w_up.astype(BF16), ffn_w_down.astype(BF16))
    moe_w = (moe_w_gate.astype(BF16), moe_w_up.astype(BF16), moe_w_down.astype(BF16))
    routers = []
    for m in range(moe_router.shape[0]):
        wr = jnp.pad(moe_router[m], ((0, 0), (0, LANES - N_EXPERTS)))
        wr_hi = wr.astype(BF16)
        routers.append((wr_hi, (wr - wr_hi.astype(F32)).astype(BF16)))

    def run(x):
        b, s, _ = x.shape
        t = b * s
        xf = x.reshape(t, D_MODEL)
        for l in range(DEPTH):
            lw = layers[l]
            mq, mk, mvt, gq, gk, gvt, dq, dk, dvt = _prep(xf, tables, lw, s)
            r3 = lambda a: a.reshape(b, s, a.shape[-1])
            o_m = _attention(r3(mq), r3(mk), mvt, mode='mla')
            o_g = _attention(r3(gq), r3(gk), gvt, mode='gqa')
            lam_init = 0.8 - 0.6 * math.exp(-0.3 * l)
            o_d = _attention(r3(dq), r3(dk), dvt, mode='diff',
                             lam_params=diff_lambda[l].astype(F32),
                             gain=jnp.tile(diff_out_norm[l], 2).reshape(1, LANES),
                             lam_init=lam_init)
            mix = (o_m.reshape(t, -1), o_g.reshape(t, -1), o_d.reshape(t, -1))
            g1, b1 = ln1_g[l].reshape(1, -1), ln1_b[l].reshape(1, -1)
            g2, b2 = ln2_g[l].reshape(1, -1), ln2_b[l].reshape(1, -1)
            m = l // 2
            if l % 2 == 0:
                x1 = _outproj(xf, *mix, lw, g1, b1)
                xf = _ffn(x1, ffn_w[0][m], ffn_w[1][m], ffn_w[2][m], g2, b2)
            else:
                x1, ri, rw = _outproj(xf, *mix, lw, g1, b1, router=routers[m])
                plan = _route_plan(ri, FFN_TM)
                xs = _dispatch(plan, x1, FFN_TM)
                ys = _grouped_ffn(plan, xs, moe_w[0][m], moe_w[1][m], moe_w[2][m], FFN_TM)
                xf = _combine(plan, x1, rw, ys, g2, b2)
        return xf.reshape(b, s, D_MODEL)

    return (run(x_prompt), run(x_sample))
```

```python
import functools
import math

import jax
import jax.numpy as jnp
from jax import lax
from jax.experimental import pallas as pl
from jax.experimental.pallas import tpu as pltpu

D_MODEL = 1024
DEPTH = 2
GRID_W = 64
ROPE_THETA = 10000.0
MLA_HEADS = 6
MLA_Q_RANK = 256
MLA_KV_RANK = 128
MLA_NOPE = 64
MLA_ROPE = 32
MLA_V = 64
GQA_HEADS = 6
GQA_KV_HEADS = 2
GQA_DIM = 64
DIFF_HEADS = 4
DIFF_QK = 32
DIFF_V = 64
D_FF = 2816
N_EXPERTS = 8
ALPHA = (2 * DEPTH) ** 0.25
LN_EPS = 1e-5
RMS_EPS = 1e-6

LOG2E = math.log2(math.e)
MLA_SCALE = (MLA_NOPE + MLA_ROPE) ** -0.5 * LOG2E
GQA_SCALE = GQA_DIM ** -0.5 * LOG2E
DIFF_SCALE = DIFF_QK ** -0.5 * LOG2E

LANES = 128
VMEM_LIMIT = 56 * 1024 * 1024
MAX_SEQ = 4096

PREP_TM = 512
ATTN_SCRATCH_BYTES = 32 * 1024 * 1024
ATTN_SB_ELEMS = 16 * 1024
ATTN_RB = 1024
PIPE_LAG = 2
ONES_ROWS = 16
VT_ROWS = LANES + ONES_ROWS
PROJ_TM = 1024
FFN_TM = 512
FFN_TF = 1408
COMBINE_TM = 256
DMA_UNROLL = 16

BF16 = jnp.bfloat16
F32 = jnp.float32

_SEG = dict(cq=(0, 256), ckv=(256, 384), kr=(384, 512), krr=(512, 640),
            gq=(640, 1024), gqr=(1024, 1408), gk=(1408, 1536), gkr=(1536, 1664),
            gv=(1664, 1792), dq=(1792, 2048), dqr=(2048, 2304), dk=(2304, 2560),
            dkr=(2560, 2816), dv=(2816, 3072))
PREP_N = 3072
GQA_ORDER = (0, 3, 1, 4, 2, 5)


def _rot_cols(w, half=16):
    k, n = w.shape
    w4 = w.reshape(k, n // (2 * half), 2, half)
    return jnp.stack([-w4[:, :, 1, :], w4[:, :, 0, :]], axis=2).reshape(k, n)


def _rope_tables():
    pos = jnp.arange(MAX_SEQ, dtype=F32)
    inv = ROPE_THETA ** (-jnp.arange(0, 32, 2, dtype=F32) / 32)

    def cs(p):
        ang = p[:, None] * inv[None, :]
        c, s = jnp.cos(ang), jnp.sin(ang)
        return jnp.concatenate([c, c], -1), jnp.concatenate([s, s], -1)

    c32, s32 = cs(pos)
    row = jnp.floor(pos / GRID_W)
    col = pos - row * GRID_W
    cr, sr = cs(row)
    cc, sc = cs(col)
    ones = jnp.ones((MAX_SEQ, 64), F32)
    z64 = jnp.zeros((MAX_SEQ, 64), F32)
    z32 = jnp.zeros((MAX_SEQ, 32), F32)
    a_c = jnp.concatenate([ones, c32, z32], -1)
    a_s = jnp.concatenate([z64, s32, z32], -1)
    p_c = jnp.tile(c32, (1, 4))
    p_s = jnp.tile(s32, (1, 4))
    x_c = jnp.tile(jnp.concatenate([cr, cc], -1), (1, 2))
    x_s = jnp.tile(jnp.concatenate([sr, sc], -1), (1, 2))
    return a_c, a_s, p_c, p_s, x_c, x_s


def _layer_weights(l, w_in, mla_q_norm, mla_w_qb, mla_kv_norm, mla_w_kvb, gqa_q_norm,
                   gqa_k_norm, w_out):
    w = w_in[l]
    o = [0, 256, 384, 416, 800, 928, 1056, 1312, 1568, 1824]
    cq, ckv, kr, gq, gk, gv, dq, dk, dv = [w[:, o[i]:o[i + 1]] for i in range(9)]
    z = lambda n: jnp.zeros((D_MODEL, n), F32)
    kr_p = jnp.concatenate([z(64), kr, z(32)], -1)
    krr_p = jnp.concatenate([z(64), _rot_cols(kr), z(32)], -1)
    order = jnp.asarray(GQA_ORDER)
    gq_p = gq.reshape(D_MODEL, GQA_HEADS, GQA_DIM)[:, order, :].reshape(D_MODEL, -1)
    w_all = jnp.concatenate(
        [cq, ckv, kr_p, krr_p, gq_p, _rot_cols(gq_p), gk, _rot_cols(gk), gv,
         dq, _rot_cols(dq), dk, _rot_cols(dk), dv], axis=-1).astype(BF16)

    qb = mla_w_qb[l].reshape(MLA_Q_RANK, MLA_HEADS, MLA_NOPE + MLA_ROPE)
    qb_nope, qb_rope = qb[:, :, :MLA_NOPE], qb[:, :, MLA_NOPE:]
    qb_rope_rot = _rot_cols(qb_rope.reshape(MLA_Q_RANK, -1)).reshape(qb_rope.shape)
    zq = lambda n: jnp.zeros((MLA_Q_RANK, MLA_HEADS, n), F32)
    w_qb = jnp.concatenate(
        [jnp.concatenate([qb_nope, qb_rope, zq(32)], -1).reshape(MLA_Q_RANK, -1),
         jnp.concatenate([zq(64), qb_rope_rot, zq(32)], -1).reshape(MLA_Q_RANK, -1)],
        axis=-1).astype(BF16)

    kvb = mla_w_kvb[l].reshape(MLA_KV_RANK, MLA_HEADS, MLA_NOPE + MLA_V)
    w_kn = jnp.concatenate(
        [kvb[:, :, :MLA_NOPE], jnp.zeros((MLA_KV_RANK, MLA_HEADS, 64), F32)],
        -1).reshape(MLA_KV_RANK, -1).astype(BF16)
    w_v = kvb[:, :, MLA_NOPE:].reshape(MLA_KV_RANK, -1).astype(BF16)

    def swap16(g):
        return g.reshape(2, 2, 16)[:, ::-1, :].reshape(64)

    gqn, gkn = gqa_q_norm[l], gqa_k_norm[l]
    vecs = jnp.zeros((8, 256), F32)
    vecs = vecs.at[0, :].set(mla_q_norm[l])
    vecs = vecs.at[1, :128].set(mla_kv_norm[l])
    vecs = vecs.at[2, :128].set(jnp.tile(gqn, 2))
    vecs = vecs.at[3, :128].set(jnp.tile(swap16(gqn), 2))
    vecs = vecs.at[4, :128].set(jnp.tile(gkn, 2))
    vecs = vecs.at[5, :128].set(jnp.tile(swap16(gkn), 2))

    wo = w_out[l]
    wo_g = wo[384:768].reshape(GQA_HEADS, GQA_DIM, D_MODEL)[order, :, :].reshape(384, D_MODEL)
    wo = jnp.concatenate([wo[:384], wo_g, wo[768:]], axis=0).astype(BF16)
    return dict(w_all=w_all, w_qb=w_qb, w_kn=w_kn, w_v=w_v, vecs=vecs, wo=wo)


def _store_transposed(vt_ref, v):
    tm = v.shape[0]
    for p in range(v.shape[1] // LANES):
        vt_ref[p * VT_ROWS:p * VT_ROWS + LANES, :] = (
            v[:, p * LANES:(p + 1) * LANES].T.astype(BF16))
        vt_ref[p * VT_ROWS + LANES:(p + 1) * VT_ROWS, :] = jnp.ones((ONES_ROWS, tm), BF16)


def _prep_kernel(x_ref, ac_ref, as_ref, pc_ref, ps_ref, xc_ref, xs_ref,
                 wall_ref, wqb_ref, wkn_ref, wv_ref, vec_ref,
                 mq_ref, mk_ref, mvt_ref, gq_ref, gk_ref, gvt_ref, dq_ref, dk_ref, dvt_ref):
    tm = x_ref.shape[0]
    xb = x_ref[...].astype(BF16)
    h = jnp.dot(xb, wall_ref[...], preferred_element_type=F32)
    seg = lambda name: h[:, _SEG[name][0]:_SEG[name][1]]
    a_c, a_s = ac_ref[...], as_ref[...]
    p_c, p_s = pc_ref[...], ps_ref[...]
    x_c, x_s = xc_ref[...], xs_ref[...]

    cq = seg('cq')
    cqn = cq * lax.rsqrt(jnp.mean(cq * cq, axis=-1, keepdims=True) + RMS_EPS)
    cqb = (cqn * vec_ref[0:1, :]).astype(BF16)
    q2 = jnp.dot(cqb, wqb_ref[...], preferred_element_type=F32)
    ckv = seg('ckv')
    ckvn = ckv * lax.rsqrt(jnp.mean(ckv * ckv, axis=-1, keepdims=True) + RMS_EPS)
    ckvb = (ckvn * vec_ref[1:2, 0:128]).astype(BF16)
    kn = jnp.dot(ckvb, wkn_ref[...], preferred_element_type=F32)
    kro = seg('kr') * a_c + seg('krr') * a_s
    for hd in range(MLA_HEADS):
        sl = slice(hd * LANES, (hd + 1) * LANES)
        sr = slice(768 + hd * LANES, 768 + (hd + 1) * LANES)
        mq_ref[:, sl] = ((q2[:, sl] * a_c + q2[:, sr] * a_s) * MLA_SCALE).astype(BF16)
        mk_ref[:, sl] = (kn[:, sl] + kro).astype(BF16)
    _store_transposed(mvt_ref, jnp.dot(ckvb, wv_ref[...], preferred_element_type=F32))

    lane = lax.broadcasted_iota(jnp.int32, (tm, LANES), 1)
    lo = lane < 64

    def half_rms(z):
        zz = z * z
        s_lo = jnp.sum(jnp.where(lo, zz, 0.0), axis=-1, keepdims=True)
        s_hi = jnp.sum(jnp.where(lo, 0.0, zz), axis=-1, keepdims=True)
        return jnp.where(lo, lax.rsqrt(s_lo * (1.0 / 64) + RMS_EPS),
                         lax.rsqrt(s_hi * (1.0 / 64) + RMS_EPS))

    gq, gqr = seg('gq'), seg('gqr')
    g_q, g_qs = vec_ref[2:3, 0:128], vec_ref[3:4, 0:128]
    for j in range(3):
        sl = slice(j * LANES, (j + 1) * LANES)
        z, zr = gq[:, sl], gqr[:, sl]
        r = half_rms(z)
        gq_ref[:, sl] = (r * ((z * g_q) * x_c + (zr * g_qs) * x_s) * GQA_SCALE).astype(BF16)
    z, zr = seg('gk'), seg('gkr')
    g_k, g_ks = vec_ref[4:5, 0:128], vec_ref[5:6, 0:128]
    gk_ref[...] = (half_rms(z) * ((z * g_k) * x_c + (zr * g_ks) * x_s)).astype(BF16)
    _store_transposed(gvt_ref, seg('gv'))

    dq, dqr, dk, dkr = seg('dq'), seg('dqr'), seg('dk'), seg('dkr')
    for j in range(2):
        sl = slice(j * LANES, (j + 1) * LANES)
        dq_ref[:, sl] = ((dq[:, sl] * p_c + dqr[:, sl] * p_s) * DIFF_SCALE).astype(BF16)
        dk_ref[:, sl] = (dk[:, sl] * p_c + dkr[:, sl] * p_s).astype(BF16)
    _store_transposed(dvt_ref, seg('dv'))


def _prep(x, tables, lw, seq):
    t = x.shape[0]
    tm = PREP_TM
    nblk = seq // tm
    row = lambda w: pl.BlockSpec((tm, w), lambda i: (i, 0))
    tab = pl.BlockSpec((tm, LANES), lambda i: (i % nblk, 0))
    full = lambda a: pl.BlockSpec(a.shape, lambda i: (0, 0), pipeline_mode=pl.Buffered(1))
    col = lambda pairs: pl.BlockSpec((pairs * VT_ROWS, tm), lambda i: (0, i))
    outs = ((768, row), (768, row), (3, col), (384, row), (128, row), (1, col),
            (256, row), (256, row), (2, col))
    return pl.pallas_call(
        _prep_kernel,
        grid=(t // tm,),
        in_specs=[row(D_MODEL)] + [tab] * 6
        + [full(lw['w_all']), full(lw['w_qb']), full(lw['w_kn']), full(lw['w_v']),
           full(lw['vecs'])],
        out_specs=[spec(w) for w, spec in outs],
        out_shape=[jax.ShapeDtypeStruct((w * VT_ROWS, t) if spec is col else (t, w), BF16)
                   for w, spec in outs],
        compiler_params=pltpu.CompilerParams(
            dimension_semantics=("parallel",), vmem_limit_bytes=VMEM_LIMIT),
        name="mixer_prep",
    )(x, *tables, lw['w_all'], lw['w_qb'], lw['w_kn'], lw['w_v'], lw['vecs'])


_NT = (((1,), (1,)), ((), ()))


def _normalised(oe):
    return oe[:LANES, :] * (1.0 / oe[LANES:LANES + 1, :])


def _lane_mask(q, keep):
    return jnp.where(keep, q.astype(F32), 0.0).astype(BF16)


def _stage_scores_numerators(q_ref, k_ref, st, m_sc, e_old, mode, head):
    q = q_ref[0]
    tq = q.shape[0]
    n_keys = k_ref.shape[1]
    lane = lax.broadcasted_iota(jnp.int32, (tq, LANES), 1)
    lo_l, hi_l = slice(0, LANES), slice(LANES, 2 * LANES)
    if mode == 'mla':
        ops = ((lo_l, q[:, :LANES]), (hi_l, q[:, LANES:]))
    elif mode == 'gqa':
        ops = ((lo_l, _lane_mask(q, lane < 64)), (lo_l, _lane_mask(q, lane >= 64)))
    else:
        lo = 64 * head
        ops = tuple((lo_l, _lane_mask(q, (lane >= lo + 32 * j) & (lane < lo + 32 * j + 32)))
                    for j in range(2))
    sb = ATTN_SB_ELEMS // tq
    for j, (k_lanes, q_op) in enumerate(ops):
        m_run = None
        m_o = m_sc[j:j + 1, :]
        for c in range(n_keys // ATTN_RB):
            rows = slice(c * ATTN_RB, (c + 1) * ATTN_RB)
            for r in range(ATTN_RB // sb):
                rr = slice(c * ATTN_RB + r * sb, c * ATTN_RB + (r + 1) * sb)
                e_old[j][rr, :] = jnp.exp2(st[j][rr, :] - m_o).astype(BF16)
            s_c = lax.dot_general(k_ref[0, rows, k_lanes], q_op, _NT,
                                  preferred_element_type=F32)
            st[j][rows, :] = s_c
            cm = jnp.max(s_c, axis=0, keepdims=True)
            m_run = cm if m_run is None else jnp.maximum(m_run, cm)
        m_sc[j:j + 1, :] = m_run


def _stage_pv(vt_ref, e, o_ref, mode, head, lam_ref, gain_ref, lam_init):
    vt = vt_ref[...]
    pv = [_normalised(jnp.dot(vt, e[j][...], preferred_element_type=F32)) for j in range(2)]
    row = lax.broadcasted_iota(jnp.int32, pv[0].shape, 0)
    if mode != 'diff':
        o_ref[0] = jnp.where(row < 64, pv[0], pv[1]).T.astype(o_ref.dtype)
        return
    lp = lam_ref[...]
    lam = (jnp.exp(jnp.sum(lp[0:1, :] * lp[1:2, :], axis=-1, keepdims=True))
           - jnp.exp(jnp.sum(lp[2:3, :] * lp[3:4, :], axis=-1, keepdims=True)) + lam_init)
    mine = (row >= 64 * head) & (row < 64 * head + 64)
    o = jnp.where(mine, pv[0] - lam * pv[1], 0.0)
    ms = jnp.sum(o * o, axis=0, keepdims=True) * (1.0 / DIFF_V)
    new = (o * lax.rsqrt(ms + RMS_EPS)).T * gain_ref[...] * (1.0 - lam_init)
    if head == 0:
        o_ref[0] = new.astype(o_ref.dtype)
    else:
        lane = lax.broadcasted_iota(jnp.int32, new.shape, 1)
        o_ref[0] = jnp.where(lane >= 64, new, o_ref[0].astype(F32)).astype(o_ref.dtype)


def _attn_kernel(*refs, mode, lam_init):
    if mode == 'diff':
        lam_ref, gain_ref, refs = refs[0], refs[1], refs[2:]
    else:
        lam_ref = gain_ref = None
    q_ref, k_ref, vt_ref, o_ref, st0, st1, e00, e01, e10, e11, m_sc = refs
    step = pl.program_id(0)
    st, e = (st0, st1), ((e00, e01), (e10, e11))

    @pl.when(step == 0)
    def _():
        for buf in (st0, st1, m_sc):
            buf[...] = jnp.zeros_like(buf)
        for buf in (e00, e01, e10, e11):
            buf[...] = jnp.ones_like(buf)

    for par in range(2):
        @pl.when(step % 2 == par)
        def _():
            _stage_scores_numerators(q_ref, k_ref, st, m_sc, e[1 - par], mode, par)
            _stage_pv(vt_ref, e[par], o_ref, mode, par, lam_ref, gain_ref, lam_init)


def _attention(q, k, vt, *, mode, lam_params=None, gain=None, lam_init=None):
    b, s, _ = q.shape
    tq = ATTN_SCRATCH_BYTES // ((2 * 4 + 4 * 2) * s)
    if mode == 'mla':
        npairs, qw, kw, per_tile = 3, 2 * LANES, 2 * LANES, 1
        kmap = lambda bi, p, i: (bi, 0, p)
        vmap = lambda bi, p, i: (p, bi)
    elif mode == 'gqa':
        npairs, qw, kw, per_tile = 3, LANES, LANES, 1
        kmap = lambda bi, p, i: (bi, 0, 0)
        vmap = lambda bi, p, i: (0, bi)
    else:
        npairs, qw, kw, per_tile = 2, LANES, LANES, 2
        kmap = lambda bi, p, i: (bi, 0, p)
        vmap = lambda bi, p, i: (p, bi)
    qmap = lambda bi, p, i: (bi, i, p)
    nq = s // tq
    n_units = b * npairs * nq * per_tile

    def lagged(index_map, lag):
        def wrapped(step):
            tile = jnp.clip(step - lag, 0, n_units - 1) // per_tile
            return index_map(tile // (nq * npairs), (tile // nq) % npairs, tile % nq)
        return wrapped

    in_specs = [pl.BlockSpec((1, tq, qw), lagged(qmap, 0)),
                pl.BlockSpec((1, s, kw), lagged(kmap, 0)),
                pl.BlockSpec((VT_ROWS, s), lagged(vmap, PIPE_LAG))]
    args = [q, k, vt]
    if mode == 'diff':
        in_specs = [pl.BlockSpec(lam_params.shape, lambda step: (0, 0)),
                    pl.BlockSpec(gain.shape, lambda step: (0, 0))] + in_specs
        args = [lam_params, gain] + args
    return pl.pallas_call(
        functools.partial(_attn_kernel, mode=mode, lam_init=lam_init),
        grid=(n_units + PIPE_LAG,),
        in_specs=in_specs,
        out_specs=pl.BlockSpec((1, tq, LANES), lagged(qmap, PIPE_LAG)),
        out_shape=jax.ShapeDtypeStruct((b, s, npairs * LANES), BF16),
        scratch_shapes=([pltpu.VMEM((s, tq), F32)] * 2 + [pltpu.VMEM((s, tq), BF16)] * 4
                        + [pltpu.VMEM((8, tq), F32)]),
        compiler_params=pltpu.CompilerParams(
            dimension_semantics=("arbitrary",), vmem_limit_bytes=VMEM_LIMIT),
        name="attn_" + mode,
    )(*args)


def _layernorm(z, g, b):
    mu = jnp.mean(z, axis=-1, keepdims=True)
    zc = z - mu
    var = jnp.mean(zc * zc, axis=-1, keepdims=True)
    return zc * lax.rsqrt(var + LN_EPS) * g + b


def _router_top2(x, wr_hi, wr_lo):
    x_hi = x.astype(BF16)
    x_lo = (x - x_hi.astype(F32)).astype(BF16)
    logits = (jnp.dot(x_hi, wr_hi, preferred_element_type=F32)
              + jnp.dot(x_lo, wr_hi, preferred_element_type=F32)
              + jnp.dot(x_hi, wr_lo, preferred_element_type=F32))
    l8 = logits.T[0:N_EXPERTS, :]
    row = lax.broadcasted_iota(jnp.int32, l8.shape, 0)
    neg = -jnp.inf
    m1 = jnp.max(l8, axis=0, keepdims=True)
    i1 = jnp.min(jnp.where(l8 == m1, row, N_EXPERTS), axis=0, keepdims=True)
    rest = jnp.where(row == i1, neg, l8)
    m2 = jnp.max(rest, axis=0, keepdims=True)
    i2 = jnp.min(jnp.where(rest == m2, row, N_EXPERTS), axis=0, keepdims=True)
    e2 = jnp.exp(m2 - m1)
    w1 = 1.0 / (1.0 + e2)
    ri = jnp.where(row == 0, i1, jnp.where(row == 1, i2, 0))
    rw = jnp.where(row == 0, w1, jnp.where(row == 1, e2 * w1, 0.0))
    pad = lambda a: jnp.concatenate(
        [a, jnp.zeros((LANES - N_EXPERTS, a.shape[1]), a.dtype)], axis=0).T
    return pad(ri), pad(rw)


def _outproj_kernel(*refs, route):
    if route:
        (x_ref, om_ref, og_ref, od_ref, w_ref, g_ref, b_ref,
         rhi_ref, rlo_ref, o_ref, ri_ref, rw_ref) = refs
    else:
        x_ref, om_ref, og_ref, od_ref, w_ref, g_ref, b_ref, o_ref = refs
    mix = jnp.concatenate([om_ref[...], og_ref[...], od_ref[...]], axis=-1)
    y = jnp.dot(mix, w_ref[...], preferred_element_type=F32)
    x1 = _layernorm(ALPHA * x_ref[...] + y, g_ref[...], b_ref[...])
    o_ref[...] = x1
    if route:
        ri_ref[...], rw_ref[...] = _router_top2(x1, rhi_ref[...], rlo_ref[...])


def _outproj(x, o_m, o_g, o_d, lw, g, b, router=None):
    t = x.shape[0]
    tm = PROJ_TM
    route = router is not None
    row = lambda w: pl.BlockSpec((tm, w), lambda i: (i, 0))
    full = lambda a: pl.BlockSpec(a.shape, lambda i: (0, 0))
    in_specs = [row(D_MODEL), row(384), row(384), row(256), full(lw['wo']), full(g), full(b)]
    args = [x, o_m, o_g, o_d, lw['wo'], g, b]
    out_specs = [row(D_MODEL)]
    out_shape = [jax.ShapeDtypeStruct((t, D_MODEL), F32)]
    if route:
        in_specs += [full(router[0]), full(router[1])]
        args += list(router)
        out_specs += [row(LANES), row(LANES)]
        out_shape += [jax.ShapeDtypeStruct((t, LANES), jnp.int32),
                      jax.ShapeDtypeStruct((t, LANES), F32)]
    outs = pl.pallas_call(
        functools.partial(_outproj_kernel, route=route),
        grid=(t // tm,),
        in_specs=in_specs,
        out_specs=out_specs,
        out_shape=out_shape,
        compiler_params=pltpu.CompilerParams(
            dimension_semantics=("parallel",), vmem_limit_bytes=VMEM_LIMIT),
        name="outproj_ln_route" if route else "outproj_ln",
    )(*args)
    return outs if route else outs[0]


def _swiglu_block(xb, wg, wu, wd):
    gt = jnp.dot(xb, wg, preferred_element_type=F32)
    up = jnp.dot(xb, wu, preferred_element_type=F32)
    act = (gt / (1.0 + jnp.exp(-gt)) * up).astype(BF16)
    return jnp.dot(act, wd, preferred_element_type=F32)


def _ffn_kernel(x_ref, wg_ref, wu_ref, wd_ref, g_ref, b_ref, o_ref, xb_sc, acc_sc):
    f = pl.program_id(1)

    @pl.when(f == 0)
    def _():
        xb_sc[...] = x_ref[...].astype(BF16)
        acc_sc[...] = jnp.zeros_like(acc_sc)

    acc_sc[...] += _swiglu_block(xb_sc[...], wg_ref[...], wu_ref[...], wd_ref[...])

    @pl.when(f == pl.num_programs(1) - 1)
    def _():
        o_ref[...] = _layernorm(ALPHA * x_ref[...] + acc_sc[...], g_ref[...], b_ref[...])


def _ffn(x, wg, wu, wd, g, b):
    t = x.shape[0]
    tm, tf = FFN_TM, D_FF
    row = pl.BlockSpec((tm, D_MODEL), lambda i, f: (i, 0))
    full = lambda a: pl.BlockSpec(a.shape, lambda i, f: (0, 0))
    once = pl.Buffered(1)
    return pl.pallas_call(
        _ffn_kernel,
        grid=(t // tm, D_FF // tf),
        in_specs=[row,
                  pl.BlockSpec((D_MODEL, tf), lambda i, f: (0, f), pipeline_mode=once),
                  pl.BlockSpec((D_MODEL, tf), lambda i, f: (0, f), pipeline_mode=once),
                  pl.BlockSpec((tf, D_MODEL), lambda i, f: (f, 0), pipeline_mode=once),
                  full(g), full(b)],
        out_specs=row,
        out_shape=jax.ShapeDtypeStruct((t, D_MODEL), F32),
        scratch_shapes=[pltpu.VMEM((tm, D_MODEL), BF16), pltpu.VMEM((tm, D_MODEL), F32)],
        compiler_params=pltpu.CompilerParams(
            dimension_semantics=("parallel", "arbitrary"), vmem_limit_bytes=VMEM_LIMIT),
        name="ffn_ln",
    )(x, wg, wu, wd, g, b)


def _route_plan(ri, tm):
    ea = ri[:, :2].reshape(-1)
    n_assign = ea.shape[0]
    experts = jnp.arange(N_EXPERTS, dtype=jnp.int32)
    oh = (ea[:, None] == experts[None, :]).astype(jnp.int32)
    csum = jnp.cumsum(oh, axis=0)
    rank = jnp.sum((csum - oh) * oh, axis=1)
    tiles = (csum[-1] + tm - 1) // tm
    tile_end = jnp.cumsum(tiles)
    start = (tile_end - tiles) * tm
    dest = jnp.sum(oh * start[None, :], axis=1) + rank
    n_tiles = n_assign // tm + N_EXPERTS
    tile_ids = jnp.arange(n_tiles, dtype=jnp.int32)
    tile_expert = jnp.sum((tile_ids[:, None] >= tile_end[None, :]).astype(jnp.int32), axis=1)
    last_used = jnp.max(jnp.where(tiles > 0, experts, 0))
    tile_expert = jnp.minimum(tile_expert, last_used).astype(jnp.int32)
    return dict(dest=dest.astype(jnp.int32), tile_expert=tile_expert,
                n_valid=tile_end[-1:].astype(jnp.int32), n_tiles=n_tiles)


def _tile_dest(plan, t, tm):
    return plan['dest'].reshape(t // tm, tm, 2).transpose(0, 2, 1).reshape(t // tm, 1, 2 * tm)


def _row_copy(src, src_row, dst, dst_row, sem):
    return pltpu.make_async_copy(src.at[pl.ds(src_row, 1)], dst.at[pl.ds(dst_row, 1)],
                                 sem.at[0])


def _dispatch_kernel(dest_ref, x_ref, init_hbm, o_hbm, sem):
    del init_hbm
    tm = x_ref.shape[0]

    def issue(j, c):
        _row_copy(x_ref, j, o_hbm, dest_ref[0, 0, j], sem).start()
        _row_copy(x_ref, j, o_hbm, dest_ref[0, 0, tm + j], sem).start()
        return c

    lax.fori_loop(0, tm, issue, 0, unroll=DMA_UNROLL // 2)
    for _ in range(2):
        pltpu.make_async_copy(x_ref, o_hbm.at[pl.ds(0, tm)], sem.at[0]).wait()


def _dispatch(plan, x, row_tile):
    t = x.shape[0]
    tm = COMBINE_TM
    n = plan['n_tiles'] * row_tile
    return pl.pallas_call(
        _dispatch_kernel,
        grid=(t // tm,),
        in_specs=[pl.BlockSpec((1, 1, 2 * tm), lambda i: (i, 0, 0), memory_space=pltpu.SMEM),
                  pl.BlockSpec((tm, D_MODEL), lambda i: (i, 0)),
                  pl.BlockSpec(memory_space=pl.ANY)],
        out_specs=pl.BlockSpec(memory_space=pl.ANY),
        out_shape=jax.ShapeDtypeStruct((n, D_MODEL), x.dtype),
        input_output_aliases={2: 0},
        scratch_shapes=[pltpu.SemaphoreType.DMA((1,))],
        compiler_params=pltpu.CompilerParams(dimension_semantics=("arbitrary",)),
        name="moe_dispatch",
    )(_tile_dest(plan, t, tm), x, jnp.zeros((n, D_MODEL), x.dtype))


def _grouped_ffn_kernel(te_ref, nv_ref, x_ref, wg_ref, wu_ref, wd_ref, o_ref, xb_sc, acc_sc):
    i, f = pl.program_id(0), pl.program_id(1)
    last = f == pl.num_programs(1) - 1
    valid = i < nv_ref[0]

    @pl.when(valid)
    def _():
        @pl.when(f == 0)
        def _():
            xb_sc[...] = x_ref[...].astype(BF16)
            acc_sc[...] = jnp.zeros_like(acc_sc)

        acc_sc[...] += _swiglu_block(xb_sc[...], wg_ref[0], wu_ref[0], wd_ref[0])

        @pl.when(last)
        def _():
            o_ref[...] = acc_sc[...]

    @pl.when(jnp.logical_and(jnp.logical_not(valid), last))
    def _():
        o_ref[...] = jnp.zeros_like(o_ref)


def _grouped_ffn(plan, xs, wg, wu, wd, tm):
    tf = D_FF
    row = pl.BlockSpec((tm, D_MODEL), lambda i, f, te, nv: (i, 0))
    once = pl.Buffered(1)
    return pl.pallas_call(
        _grouped_ffn_kernel,
        grid_spec=pltpu.PrefetchScalarGridSpec(
            num_scalar_prefetch=2,
            grid=(plan['n_tiles'], D_FF // tf),
            in_specs=[row,
                      pl.BlockSpec((1, D_MODEL, tf), lambda i, f, te, nv: (te[i], 0, f),
                                   pipeline_mode=once),
                      pl.BlockSpec((1, D_MODEL, tf), lambda i, f, te, nv: (te[i], 0, f),
                                   pipeline_mode=once),
                      pl.BlockSpec((1, tf, D_MODEL), lambda i, f, te, nv: (te[i], f, 0),
                                   pipeline_mode=once)],
            out_specs=row,
            scratch_shapes=[pltpu.VMEM((tm, D_MODEL), BF16),
                            pltpu.VMEM((tm, D_MODEL), F32)]),
        out_shape=jax.ShapeDtypeStruct(xs.shape, F32),
        compiler_params=pltpu.CompilerParams(
            dimension_semantics=("parallel", "arbitrary"), vmem_limit_bytes=VMEM_LIMIT),
        name="moe_grouped_ffn",
    )(plan['tile_expert'], plan['n_valid'], xs, wg, wu, wd)


def _combine_kernel(dest_ref, x_ref, w_ref, ys_hbm, g_ref, b_ref, o_ref, ybuf, sem):
    tm = x_ref.shape[0]

    def issue(j, c):
        _row_copy(ys_hbm, dest_ref[0, 0, j], ybuf, j, sem).start()
        return c

    lax.fori_loop(0, 2 * tm, issue, 0, unroll=DMA_UNROLL)
    pltpu.make_async_copy(ys_hbm.at[pl.ds(0, 2 * tm)], ybuf, sem.at[0]).wait()
    w = w_ref[...]
    y = w[:, 0:1] * ybuf[0:tm, :] + w[:, 1:2] * ybuf[tm:2 * tm, :]
    o_ref[...] = _layernorm(ALPHA * x_ref[...] + y, g_ref[...], b_ref[...])


def _combine(plan, x, rw, ys, g, b):
    t = x.shape[0]
    tm = COMBINE_TM
    dest = _tile_dest(plan, t, tm)
    full = lambda a: pl.BlockSpec(a.shape, lambda i: (0, 0))
    return pl.pallas_call(
        _combine_kernel,
        grid=(t // tm,),
        in_specs=[pl.BlockSpec((1, 1, 2 * tm), lambda i: (i, 0, 0), memory_space=pltpu.SMEM),
                  pl.BlockSpec((tm, D_MODEL), lambda i: (i, 0)),
                  pl.BlockSpec((tm, LANES), lambda i: (i, 0)),
                  pl.BlockSpec(memory_space=pl.ANY), full(g), full(b)],
        out_specs=pl.BlockSpec((tm, D_MODEL), lambda i: (i, 0)),
        out_shape=jax.ShapeDtypeStruct((t, D_MODEL), F32),
        scratch_shapes=[pltpu.VMEM((2 * tm, D_MODEL), F32), pltpu.SemaphoreType.DMA((1,))],
        compiler_params=pltpu.CompilerParams(dimension_semantics=("arbitrary",)),
        name="moe_combine_ln",
    )(dest, x, rw, ys, g, b)


def kernel(x_prompt, x_sample, w_in, mla_q_norm, mla_w_qb, mla_kv_norm, mla_w_kvb,
           gqa_q_norm, gqa_k_norm, diff_lambda, diff_out_norm, w_out, ln1_g, ln1_b,
           ffn_w_gate, ffn_w_up, ffn_w_down, moe_router, moe_w_gate, moe_w_up,
           moe_w_down, ln2_g, ln2_b):
    tables = _rope_tables()
    layers = [_layer_weights(l, w_in, mla_q_norm, mla_w_qb, mla_kv_norm, mla_w_kvb,
                             gqa_q_norm, gqa_k_norm, w_out) for l in range(DEPTH)]
    ffn_w = (ffn_w_gate.astype(BF16), ffn_w_up.astype(BF16), ffn_w_down.astype(BF16))
    moe_w = (moe_w_gate.astype(BF16), moe_w_up.astype(BF16), moe_w_down.astype(BF16))
    routers = []
    for m in range(moe_router.shape[0]):
        wr = jnp.pad(moe_router[m], ((0, 0), (0, LANES - N_EXPERTS)))
        wr_hi = wr.astype(BF16)
        routers.append((wr_hi, (wr - wr_hi.astype(F32)).astype(BF16)))

    def run(x):
        b, s, _ = x.shape
        t = b * s
        xf = x.reshape(t, D_MODEL)
        for l in range(DEPTH):
            lw = layers[l]
            mq, mk, mvt, gq, gk, gvt, dq, dk, dvt = _prep(xf, tables, lw, s)
            r3 = lambda a: a.reshape(b, s, a.shape[-1])
            o_m = _attention(r3(mq), r3(mk), mvt, mode='mla')
            o_g = _attention(r3(gq), r3(gk), gvt, mode='gqa')
            lam_init = 0.8 - 0.6 * math.exp(-0.3 * l)
            o_d = _attention(r3(dq), r3(dk), dvt, mode='diff',
                             lam_params=diff_lambda[l].astype(F32),
                             gain=jnp.tile(diff_out_norm[l], 2).reshape(1, LANES),
                             lam_init=lam_init)
            mix = (o_m.reshape(t, -1), o_g.reshape(t, -1), o_d.reshape(t, -1))
            g1, b1 = ln1_g[l].reshape(1, -1), ln1_b[l].reshape(1, -1)
            g2, b2 = ln2_g[l].reshape(1, -1), ln2_b[l].reshape(1, -1)
            m = l // 2
            if l % 2 == 0:
                x1 = _outproj(xf, *mix, lw, g1, b1)
                xf = _ffn(x1, ffn_w[0][m], ffn_w[1][m], ffn_w[2][m], g2, b2)
            else:
                x1, ri, rw = _outproj(xf, *mix, lw, g1, b1, router=routers[m])
                plan = _route_plan(ri, FFN_TM)
                xs = _dispatch(plan, x1, FFN_TM)
                ys = _grouped_ffn(plan, xs, moe_w[0][m], moe_w[1][m], moe_w[2][m], FFN_TM)
                xf = _combine(plan, x1, rw, ys, g2, b2)
        return xf.reshape(b, s, D_MODEL)

    return (run(x_prompt), run(x_sample))
```

```python
import functools
import math

import jax
import jax.numpy as jnp
from jax import lax
from jax.experimental import pallas as pl
from jax.experimental.pallas import tpu as pltpu

D_MODEL = 1024
DEPTH = 2
GRID_W = 64
ROPE_THETA = 10000.0
MLA_HEADS = 6
MLA_Q_RANK = 256
MLA_KV_RANK = 128
MLA_NOPE = 64
MLA_ROPE = 32
MLA_V = 64
GQA_HEADS = 6
GQA_KV_HEADS = 2
GQA_DIM = 64
DIFF_HEADS = 4
DIFF_QK = 32
DIFF_V = 64
D_FF = 2816
N_EXPERTS = 8
ALPHA = (2 * DEPTH) ** 0.25
LN_EPS = 1e-5
RMS_EPS = 1e-6

LOG2E = math.log2(math.e)
MLA_SCALE = (MLA_NOPE + MLA_ROPE) ** -0.5 * LOG2E
GQA_SCALE = GQA_DIM ** -0.5 * LOG2E
DIFF_SCALE = DIFF_QK ** -0.5 * LOG2E

LANES = 128
VMEM_LIMIT = 56 * 1024 * 1024
MAX_SEQ = 4096

PREP_TM = 512
ATTN_SCRATCH_BYTES = 32 * 1024 * 1024
ATTN_SB_ELEMS = 16 * 1024
ATTN_RB = 1024
PIPE_LAG = 2
ONES_ROWS = 16
VT_ROWS = LANES + ONES_ROWS
PROJ_TM = 1024
FFN_TM = 512
FFN_TF = 1408
COMBINE_TM = 256
DMA_UNROLL = 16

BF16 = jnp.bfloat16
F32 = jnp.float32

_SEG = dict(cq=(0, 256), ckv=(256, 384), kr=(384, 512), krr=(512, 640),
            gq=(640, 1024), gqr=(1024, 1408), gk=(1408, 1536), gkr=(1536, 1664),
            gv=(1664, 1792), dq=(1792, 2048), dqr=(2048, 2304), dk=(2304, 2560),
            dkr=(2560, 2816), dv=(2816, 3072))
PREP_N = 3072
GQA_ORDER = (0, 3, 1, 4, 2, 5)


def _rot_cols(w, half=16):
    k, n = w.shape
    w4 = w.reshape(k, n // (2 * half), 2, half)
    return jnp.stack([-w4[:, :, 1, :], w4[:, :, 0, :]], axis=2).reshape(k, n)


def _rope_tables():
    pos = jnp.arange(MAX_SEQ, dtype=F32)
    inv = ROPE_THETA ** (-jnp.arange(0, 32, 2, dtype=F32) / 32)

    def cs(p):
        ang = p[:, None] * inv[None, :]
        c, s = jnp.cos(ang), jnp.sin(ang)
        return jnp.concatenate([c, c], -1), jnp.concatenate([s, s], -1)

    c32, s32 = cs(pos)
    row = jnp.floor(pos / GRID_W)
    col = pos - row * GRID_W
    cr, sr = cs(row)
    cc, sc = cs(col)
    ones = jnp.ones((MAX_SEQ, 64), F32)
    z64 = jnp.zeros((MAX_SEQ, 64), F32)
    z32 = jnp.zeros((MAX_SEQ, 32), F32)
    a_c = jnp.concatenate([ones, c32, z32], -1)
    a_s = jnp.concatenate([z64, s32, z32], -1)
    p_c = jnp.tile(c32, (1, 4))
    p_s = jnp.tile(s32, (1, 4))
    x_c = jnp.tile(jnp.concatenate([cr, cc], -1), (1, 2))
    x_s = jnp.tile(jnp.concatenate([sr, sc], -1), (1, 2))
    return a_c, a_s, p_c, p_s, x_c, x_s


def _layer_weights(l, w_in, mla_q_norm, mla_w_qb, mla_kv_norm, mla_w_kvb, gqa_q_norm,
                   gqa_k_norm, w_out):
    w = w_in[l]
    o = [0, 256, 384, 416, 800, 928, 1056, 1312, 1568, 1824]
    cq, ckv, kr, gq, gk, gv, dq, dk, dv = [w[:, o[i]:o[i + 1]] for i in range(9)]
    z = lambda n: jnp.zeros((D_MODEL, n), F32)
    kr_p = jnp.concatenate([z(64), kr, z(32)], -1)
    krr_p = jnp.concatenate([z(64), _rot_cols(kr), z(32)], -1)
    order = jnp.asarray(GQA_ORDER)
    gq_p = gq.reshape(D_MODEL, GQA_HEADS, GQA_DIM)[:, order, :].reshape(D_MODEL, -1)
    w_all = jnp.concatenate(
        [cq, ckv, kr_p, krr_p, gq_p, _rot_cols(gq_p), gk, _rot_cols(gk), gv,
         dq, _rot_cols(dq), dk, _rot_cols(dk), dv], axis=-1).astype(BF16)

    qb = mla_w_qb[l].reshape(MLA_Q_RANK, MLA_HEADS, MLA_NOPE + MLA_ROPE)
    qb_nope, qb_rope = qb[:, :, :MLA_NOPE], qb[:, :, MLA_NOPE:]
    qb_rope_rot = _rot_cols(qb_rope.reshape(MLA_Q_RANK, -1)).reshape(qb_rope.shape)
    zq = lambda n: jnp.zeros((MLA_Q_RANK, MLA_HEADS, n), F32)
    w_qb = jnp.concatenate(
        [jnp.concatenate([qb_nope, qb_rope, zq(32)], -1).reshape(MLA_Q_RANK, -1),
         jnp.concatenate([zq(64), qb_rope_rot, zq(32)], -1).reshape(MLA_Q_RANK, -1)],
        axis=-1).astype(BF16)

    kvb = mla_w_kvb[l].reshape(MLA_KV_RANK, MLA_HEADS, MLA_NOPE + MLA_V)
    w_kn = jnp.concatenate(
        [kvb[:, :, :MLA_NOPE], jnp.zeros((MLA_KV_RANK, MLA_HEADS, 64), F32)],
        -1).reshape(MLA_KV_RANK, -1).astype(BF16)
    w_v = kvb[:, :, MLA_NOPE:].reshape(MLA_KV_RANK, -1).astype(BF16)

    def swap16(g):
        return g.reshape(2, 2, 16)[:, ::-1, :].reshape(64)

    gqn, gkn = gqa_q_norm[l], gqa_k_norm[l]
    vecs = jnp.zeros((8, 256), F32)
    vecs = vecs.at[0, :].set(mla_q_norm[l])
    vecs = vecs.at[1, :128].set(mla_kv_norm[l])
    vecs = vecs.at[2, :128].set(jnp.tile(gqn, 2))
    vecs = vecs.at[3, :128].set(jnp.tile(swap16(gqn), 2))
    vecs = vecs.at[4, :128].set(jnp.tile(gkn, 2))
    vecs = vecs.at[5, :128].set(jnp.tile(swap16(gkn), 2))

    wo = w_out[l]
    wo_g = wo[384:768].reshape(GQA_HEADS, GQA_DIM, D_MODEL)[order, :, :].reshape(384, D_MODEL)
    wo = jnp.concatenate([wo[:384], wo_g, wo[768:]], axis=0).astype(BF16)
    return dict(w_all=w_all, w_qb=w_qb, w_kn=w_kn, w_v=w_v, vecs=vecs, wo=wo)


def _store_transposed(vt_ref, v):
    tm = v.shape[0]
    for p in range(v.shape[1] // LANES):
        vt_ref[p * VT_ROWS:p * VT_ROWS + LANES, :] = (
            v[:, p * LANES:(p + 1) * LANES].T.astype(BF16))
        vt_ref[p * VT_ROWS + LANES:(p + 1) * VT_ROWS, :] = jnp.ones((ONES_ROWS, tm), BF16)


def _prep_kernel(x_ref, ac_ref, as_ref, pc_ref, ps_ref, xc_ref, xs_ref,
                 wall_ref, wqb_ref, wkn_ref, wv_ref, vec_ref,
                 mq_ref, mk_ref, mvt_ref, gq_ref, gk_ref, gvt_ref, dq_ref, dk_ref, dvt_ref):
    tm = x_ref.shape[0]
    xb = x_ref[...].astype(BF16)
    h = jnp.dot(xb, wall_ref[...], preferred_element_type=F32)
    seg = lambda name: h[:, _SEG[name][0]:_SEG[name][1]]
    a_c, a_s = ac_ref[...], as_ref[...]
    p_c, p_s = pc_ref[...], ps_ref[...]
    x_c, x_s = xc_ref[...], xs_ref[...]

    cq = seg('cq')
    cqn = cq * lax.rsqrt(jnp.mean(cq * cq, axis=-1, keepdims=True) + RMS_EPS)
    cqb = (cqn * vec_ref[0:1, :]).astype(BF16)
    q2 = jnp.dot(cqb, wqb_ref[...], preferred_element_type=F32)
    ckv = seg('ckv')
    ckvn = ckv * lax.rsqrt(jnp.mean(ckv * ckv, axis=-1, keepdims=True) + RMS_EPS)
    ckvb = (ckvn * vec_ref[1:2, 0:128]).astype(BF16)
    kn = jnp.dot(ckvb, wkn_ref[...], preferred_element_type=F32)
    kro = seg('kr') * a_c + seg('krr') * a_s
    for hd in range(MLA_HEADS):
        sl = slice(hd * LANES, (hd + 1) * LANES)
        sr = slice(768 + hd * LANES, 768 + (hd + 1) * LANES)
        mq_ref[:, sl] = ((q2[:, sl] * a_c + q2[:, sr] * a_s) * MLA_SCALE).astype(BF16)
        mk_ref[:, sl] = (kn[:, sl] + kro).astype(BF16)
    _store_transposed(mvt_ref, jnp.dot(ckvb, wv_ref[...], preferred_element_type=F32))

    lane = lax.broadcasted_iota(jnp.int32, (tm, LANES), 1)
    lo = lane < 64

    def half_rms(z):
        zz = z * z
        s_lo = jnp.sum(jnp.where(lo, zz, 0.0), axis=-1, keepdims=True)
        s_hi = jnp.sum(jnp.where(lo, 0.0, zz), axis=-1, keepdims=True)
        return jnp.where(lo, lax.rsqrt(s_lo * (1.0 / 64) + RMS_EPS),
                         lax.rsqrt(s_hi * (1.0 / 64) + RMS_EPS))

    gq, gqr = seg('gq'), seg('gqr')
    g_q, g_qs = vec_ref[2:3, 0:128], vec_ref[3:4, 0:128]
    for j in range(3):
        sl = slice(j * LANES, (j + 1) * LANES)
        z, zr = gq[:, sl], gqr[:, sl]
        r = half_rms(z)
        gq_ref[:, sl] = (r * ((z * g_q) * x_c + (zr * g_qs) * x_s) * GQA_SCALE).astype(BF16)
    z, zr = seg('gk'), seg('gkr')
    g_k, g_ks = vec_ref[4:5, 0:128], vec_ref[5:6, 0:128]
    gk_ref[...] = (half_rms(z) * ((z * g_k) * x_c + (zr * g_ks) * x_s)).astype(BF16)
    _store_transposed(gvt_ref, seg('gv'))

    dq, dqr, dk, dkr = seg('dq'), seg('dqr'), seg('dk'), seg('dkr')
    for j in range(2):
        sl = slice(j * LANES, (j + 1) * LANES)
        dq_ref[:, sl] = ((dq[:, sl] * p_c + dqr[:, sl] * p_s) * DIFF_SCALE).astype(BF16)
        dk_ref[:, sl] = (dk[:, sl] * p_c + dkr[:, sl] * p_s).astype(BF16)
    _store_transposed(dvt_ref, seg('dv'))


def _prep(x, tables, lw, seq):
    t = x.shape[0]
    tm = PREP_TM
    nblk = seq // tm
    row = lambda w: pl.BlockSpec((tm, w), lambda i: (i, 0))
    tab = pl.BlockSpec((tm, LANES), lambda i: (i % nblk, 0))
    full = lambda a: pl.BlockSpec(a.shape, lambda i: (0, 0), pipeline_mode=pl.Buffered(1))
    col = lambda pairs: pl.BlockSpec((pairs * VT_ROWS, tm), lambda i: (0, i))
    outs = ((768, row), (768, row), (3, col), (384, row), (128, row), (1, col),
            (256, row), (256, row), (2, col))
    return pl.pallas_call(
        _prep_kernel,
        grid=(t // tm,),
        in_specs=[row(D_MODEL)] + [tab] * 6
        + [full(lw['w_all']), full(lw['w_qb']), full(lw['w_kn']), full(lw['w_v']),
           full(lw['vecs'])],
        out_specs=[spec(w) for w, spec in outs],
        out_shape=[jax.ShapeDtypeStruct((w * VT_ROWS, t) if spec is col else (t, w), BF16)
                   for w, spec in outs],
        compiler_params=pltpu.CompilerParams(
            dimension_semantics=("parallel",), vmem_limit_bytes=VMEM_LIMIT),
        name="mixer_prep",
    )(x, *tables, lw['w_all'], lw['w_qb'], lw['w_kn'], lw['w_v'], lw['vecs'])


_NT = (((1,), (1,)), ((), ()))


def _normalised(oe):
    return oe[:LANES, :] * (1.0 / oe[LANES:LANES + 1, :])


def _lane_mask(q, keep):
    return jnp.where(keep, q.astype(F32), 0.0).astype(BF16)


def _stage_scores_numerators(q_ref, k_ref, st, m_sc, e_old, mode, head):
    q = q_ref[0]
    tq = q.shape[0]
    n_keys = k_ref.shape[1]
    lane = lax.broadcasted_iota(jnp.int32, (tq, LANES), 1)
    lo_l, hi_l = slice(0, LANES), slice(LANES, 2 * LANES)
    if mode == 'mla':
        ops = ((lo_l, q[:, :LANES]), (hi_l, q[:, LANES:]))
    elif mode == 'gqa':
        ops = ((lo_l, _lane_mask(q, lane < 64)), (lo_l, _lane_mask(q, lane >= 64)))
    else:
        lo = 64 * head
        ops = tuple((lo_l, _lane_mask(q, (lane >= lo + 32 * j) & (lane < lo + 32 * j + 32)))
                    for j in range(2))
    sb = ATTN_SB_ELEMS // tq
    for j, (k_lanes, q_op) in enumerate(ops):
        m_run = None
        m_o = m_sc[j:j + 1, :]
        for c in range(n_keys // ATTN_RB):
            rows = slice(c * ATTN_RB, (c + 1) * ATTN_RB)
            for r in range(ATTN_RB // sb):
                rr = slice(c * ATTN_RB + r * sb, c * ATTN_RB + (r + 1) * sb)
                e_old[j][rr, :] = jnp.exp2(st[j][rr, :] - m_o).astype(BF16)
            s_c = lax.dot_general(k_ref[0, rows, k_lanes], q_op, _NT,
                                  preferred_element_type=F32)
            st[j][rows, :] = s_c
            cm = jnp.max(s_c, axis=0, keepdims=True)
            m_run = cm if m_run is None else jnp.maximum(m_run, cm)
        m_sc[j:j + 1, :] = m_run


def _stage_pv(vt_ref, e, o_ref, mode, head, lam_ref, gain_ref, lam_init):
    vt = vt_ref[...]
    pv = [_normalised(jnp.dot(vt, e[j][...], preferred_element_type=F32)) for j in range(2)]
    row = lax.broadcasted_iota(jnp.int32, pv[0].shape, 0)
    if mode != 'diff':
        o_ref[0] = jnp.where(row < 64, pv[0], pv[1]).T.astype(o_ref.dtype)
        return
    lp = lam_ref[...]
    lam = (jnp.exp(jnp.sum(lp[0:1, :] * lp[1:2, :], axis=-1, keepdims=True))
           - jnp.exp(jnp.sum(lp[2:3, :] * lp[3:4, :], axis=-1, keepdims=True)) + lam_init)
    mine = (row >= 64 * head) & (row < 64 * head + 64)
    o = jnp.where(mine, pv[0] - lam * pv[1], 0.0)
    ms = jnp.sum(o * o, axis=0, keepdims=True) * (1.0 / DIFF_V)
    new = (o * lax.rsqrt(ms + RMS_EPS)).T * gain_ref[...] * (1.0 - lam_init)
    if head == 0:
        o_ref[0] = new.astype(o_ref.dtype)
    else:
        lane = lax.broadcasted_iota(jnp.int32, new.shape, 1)
        o_ref[0] = jnp.where(lane >= 64, new, o_ref[0].astype(F32)).astype(o_ref.dtype)


def _attn_kernel(*refs, mode, lam_init):
    if mode == 'diff':
        lam_ref, gain_ref, refs = refs[0], refs[1], refs[2:]
    else:
        lam_ref = gain_ref = None
    q_ref, k_ref, vt_ref, o_ref, st0, st1, e00, e01, e10, e11, m_sc = refs
    step = pl.program_id(0)
    st, e = (st0, st1), ((e00, e01), (e10, e11))

    @pl.when(step == 0)
    def _():
        for buf in (st0, st1, m_sc):
            buf[...] = jnp.zeros_like(buf)
        for buf in (e00, e01, e10, e11):
            buf[...] = jnp.ones_like(buf)

    for par in range(2):
        @pl.when(step % 2 == par)
        def _():
            _stage_pv(vt_ref, e[par], o_ref, mode, par, lam_ref, gain_ref, lam_init)
            _stage_scores_numerators(q_ref, k_ref, st, m_sc, e[1 - par], mode, par)


def _attention(q, k, vt, *, mode, lam_params=None, gain=None, lam_init=None):
    b, s, _ = q.shape
    tq = ATTN_SCRATCH_BYTES // ((2 * 4 + 4 * 2) * s)
    if mode == 'mla':
        npairs, qw, kw, per_tile = 3, 2 * LANES, 2 * LANES, 1
        kmap = lambda bi, p, i: (bi, 0, p)
        vmap = lambda bi, p, i: (p, bi)
    elif mode == 'gqa':
        npairs, qw, kw, per_tile = 3, LANES, LANES, 1
        kmap = lambda bi, p, i: (bi, 0, 0)
        vmap = lambda bi, p, i: (0, bi)
    else:
        npairs, qw, kw, per_tile = 2, LANES, LANES, 2
        kmap = lambda bi, p, i: (bi, 0, p)
        vmap = lambda bi, p, i: (p, bi)
    qmap = lambda bi, p, i: (bi, i, p)
    nq = s // tq
    n_units = b * npairs * nq * per_tile

    def lagged(index_map, lag):
        def wrapped(step):
            tile = jnp.clip(step - lag, 0, n_units - 1) // per_tile
            return index_map(tile // (nq * npairs), (tile // nq) % npairs, tile % nq)
        return wrapped

    in_specs = [pl.BlockSpec((1, tq, qw), lagged(qmap, 0)),
                pl.BlockSpec((1, s, kw), lagged(kmap, 0)),
                pl.BlockSpec((VT_ROWS, s), lagged(vmap, PIPE_LAG))]
    args = [q, k, vt]
    if mode == 'diff':
        in_specs = [pl.BlockSpec(lam_params.shape, lambda step: (0, 0)),
                    pl.BlockSpec(gain.shape, lambda step: (0, 0))] + in_specs
        args = [lam_params, gain] + args
    return pl.pallas_call(
        functools.partial(_attn_kernel, mode=mode, lam_init=lam_init),
        grid=(n_units + PIPE_LAG,),
        in_specs=in_specs,
        out_specs=pl.BlockSpec((1, tq, LANES), lagged(qmap, PIPE_LAG)),
        out_shape=jax.ShapeDtypeStruct((b, s, npairs * LANES), BF16),
        scratch_shapes=([pltpu.VMEM((s, tq), F32)] * 2 + [pltpu.VMEM((s, tq), BF16)] * 4
                        + [pltpu.VMEM((8, tq), F32)]),
        compiler_params=pltpu.CompilerParams(
            dimension_semantics=("arbitrary",), vmem_limit_bytes=VMEM_LIMIT),
        name="attn_" + mode,
    )(*args)


def _layernorm(z, g, b):
    mu = jnp.mean(z, axis=-1, keepdims=True)
    zc = z - mu
    var = jnp.mean(zc * zc, axis=-1, keepdims=True)
    return zc * lax.rsqrt(var + LN_EPS) * g + b


def _router_top2(x, wr_hi, wr_lo):
    x_hi = x.astype(BF16)
    x_lo = (x - x_hi.astype(F32)).astype(BF16)
    logits = (jnp.dot(x_hi, wr_hi, preferred_element_type=F32)
              + jnp.dot(x_lo, wr_hi, preferred_element_type=F32)
              + jnp.dot(x_hi, wr_lo, preferred_element_type=F32))
    l8 = logits.T[0:N_EXPERTS, :]
    row = lax.broadcasted_iota(jnp.int32, l8.shape, 0)
    neg = -jnp.inf
    m1 = jnp.max(l8, axis=0, keepdims=True)
    i1 = jnp.min(jnp.where(l8 == m1, row, N_EXPERTS), axis=0, keepdims=True)
    rest = jnp.where(row == i1, neg, l8)
    m2 = jnp.max(rest, axis=0, keepdims=True)
    i2 = jnp.min(jnp.where(rest == m2, row, N_EXPERTS), axis=0, keepdims=True)
    e2 = jnp.exp(m2 - m1)
    w1 = 1.0 / (1.0 + e2)
    ri = jnp.where(row == 0, i1, jnp.where(row == 1, i2, 0))
    rw = jnp.where(row == 0, w1, jnp.where(row == 1, e2 * w1, 0.0))
    pad = lambda a: jnp.concatenate(
        [a, jnp.zeros((LANES - N_EXPERTS, a.shape[1]), a.dtype)], axis=0).T
    return pad(ri), pad(rw)


def _outproj_kernel(*refs, route):
    if route:
        (x_ref, om_ref, og_ref, od_ref, w_ref, g_ref, b_ref,
         rhi_ref, rlo_ref, o_ref, ri_ref, rw_ref) = refs
    else:
        x_ref, om_ref, og_ref, od_ref, w_ref, g_ref, b_ref, o_ref = refs
    mix = jnp.concatenate([om_ref[...], og_ref[...], od_ref[...]], axis=-1)
    y = jnp.dot(mix, w_ref[...], preferred_element_type=F32)
    x1 = _layernorm(ALPHA * x_ref[...] + y, g_ref[...], b_ref[...])
    o_ref[...] = x1
    if route:
        ri_ref[...], rw_ref[...] = _router_top2(x1, rhi_ref[...], rlo_ref[...])


def _outproj(x, o_m, o_g, o_d, lw, g, b, router=None):
    t = x.shape[0]
    tm = PROJ_TM
    route = router is not None
    row = lambda w: pl.BlockSpec((tm, w), lambda i: (i, 0))
    full = lambda a: pl.BlockSpec(a.shape, lambda i: (0, 0))
    in_specs = [row(D_MODEL), row(384), row(384), row(256), full(lw['wo']), full(g), full(b)]
    args = [x, o_m, o_g, o_d, lw['wo'], g, b]
    out_specs = [row(D_MODEL)]
    out_shape = [jax.ShapeDtypeStruct((t, D_MODEL), F32)]
    if route:
        in_specs += [full(router[0]), full(router[1])]
        args += list(router)
        out_specs += [row(LANES), row(LANES)]
        out_shape += [jax.ShapeDtypeStruct((t, LANES), jnp.int32),
                      jax.ShapeDtypeStruct((t, LANES), F32)]
    outs = pl.pallas_call(
        functools.partial(_outproj_kernel, route=route),
        grid=(t // tm,),
        in_specs=in_specs,
        out_specs=out_specs,
        out_shape=out_shape,
        compiler_params=pltpu.CompilerParams(
            dimension_semantics=("parallel",), vmem_limit_bytes=VMEM_LIMIT),
        name="outproj_ln_route" if route else "outproj_ln",
    )(*args)
    return outs if route else outs[0]


def _swiglu_block(xb, wg, wu, wd):
    gt = jnp.dot(xb, wg, preferred_element_type=F32)
    up = jnp.dot(xb, wu, preferred_element_type=F32)
    act = (gt / (1.0 + jnp.exp(-gt)) * up).astype(BF16)
    return jnp.dot(act, wd, preferred_element_type=F32)


def _ffn_kernel(x_ref, wg_ref, wu_ref, wd_ref, g_ref, b_ref, o_ref, xb_sc, acc_sc):
    f = pl.program_id(1)

    @pl.when(f == 0)
    def _():
        xb_sc[...] = x_ref[...].astype(BF16)
        acc_sc[...] = jnp.zeros_like(acc_sc)

    acc_sc[...] += _swiglu_block(xb_sc[...], wg_ref[...], wu_ref[...], wd_ref[...])

    @pl.when(f == pl.num_programs(1) - 1)
    def _():
        o_ref[...] = _layernorm(ALPHA * x_ref[...] + acc_sc[...], g_ref[...], b_ref[...])


def _ffn(x, wg, wu, wd, g, b):
    t = x.shape[0]
    tm, tf = FFN_TM, D_FF
    row = pl.BlockSpec((tm, D_MODEL), lambda i, f: (i, 0))
    full = lambda a: pl.BlockSpec(a.shape, lambda i, f: (0, 0))
    once = pl.Buffered(1)
    return pl.pallas_call(
        _ffn_kernel,
        grid=(t // tm, D_FF // tf),
        in_specs=[row,
                  pl.BlockSpec((D_MODEL, tf), lambda i, f: (0, f), pipeline_mode=once),
                  pl.BlockSpec((D_MODEL, tf), lambda i, f: (0, f), pipeline_mode=once),
                  pl.BlockSpec((tf, D_MODEL), lambda i, f: (f, 0), pipeline_mode=once),
                  full(g), full(b)],
        out_specs=row,
        out_shape=jax.ShapeDtypeStruct((t, D_MODEL), F32),
        scratch_shapes=[pltpu.VMEM((tm, D_MODEL), BF16), pltpu.VMEM((tm, D_MODEL), F32)],
        compiler_params=pltpu.CompilerParams(
            dimension_semantics=("parallel", "arbitrary"), vmem_limit_bytes=VMEM_LIMIT),
        name="ffn_ln",
    )(x, wg, wu, wd, g, b)


def _route_plan(ri, tm):
    ea = ri[:, :2].reshape(-1)
    n_assign = ea.shape[0]
    experts = jnp.arange(N_EXPERTS, dtype=jnp.int32)
    oh = (ea[:, None] == experts[None, :]).astype(jnp.int32)
    csum = jnp.cumsum(oh, axis=0)
    rank = jnp.sum((csum - oh) * oh, axis=1)
    tiles = (csum[-1] + tm - 1) // tm
    tile_end = jnp.cumsum(tiles)
    start = (tile_end - tiles) * tm
    dest = jnp.sum(oh * start[None, :], axis=1) + rank
    n_tiles = n_assign // tm + N_EXPERTS
    tile_ids = jnp.arange(n_tiles, dtype=jnp.int32)
    tile_expert = jnp.sum((tile_ids[:, None] >= tile_end[None, :]).astype(jnp.int32), axis=1)
    last_used = jnp.max(jnp.where(tiles > 0, experts, 0))
    tile_expert = jnp.minimum(tile_expert, last_used).astype(jnp.int32)
    return dict(dest=dest.astype(jnp.int32), tile_expert=tile_expert,
                n_valid=tile_end[-1:].astype(jnp.int32), n_tiles=n_tiles)


def _tile_dest(plan, t, tm):
    return plan['dest'].reshape(t // tm, tm, 2).transpose(0, 2, 1).reshape(t // tm, 1, 2 * tm)


def _row_copy(src, src_row, dst, dst_row, sem):
    return pltpu.make_async_copy(src.at[pl.ds(src_row, 1)], dst.at[pl.ds(dst_row, 1)],
                                 sem.at[0])


def _dispatch_kernel(dest_ref, x_ref, init_hbm, o_hbm, sem):
    del init_hbm
    tm = x_ref.shape[0]

    def issue(j, c):
        _row_copy(x_ref, j, o_hbm, dest_ref[0, 0, j], sem).start()
        _row_copy(x_ref, j, o_hbm, dest_ref[0, 0, tm + j], sem).start()
        return c

    lax.fori_loop(0, tm, issue, 0, unroll=DMA_UNROLL // 2)
    for _ in range(2):
        pltpu.make_async_copy(x_ref, o_hbm.at[pl.ds(0, tm)], sem.at[0]).wait()


def _dispatch(plan, x, row_tile):
    t = x.shape[0]
    tm = COMBINE_TM
    n = plan['n_tiles'] * row_tile
    return pl.pallas_call(
        _dispatch_kernel,
        grid=(t // tm,),
        in_specs=[pl.BlockSpec((1, 1, 2 * tm), lambda i: (i, 0, 0), memory_space=pltpu.SMEM),
                  pl.BlockSpec((tm, D_MODEL), lambda i: (i, 0)),
                  pl.BlockSpec(memory_space=pl.ANY)],
        out_specs=pl.BlockSpec(memory_space=pl.ANY),
        out_shape=jax.ShapeDtypeStruct((n, D_MODEL), x.dtype),
        input_output_aliases={2: 0},
        scratch_shapes=[pltpu.SemaphoreType.DMA((1,))],
        compiler_params=pltpu.CompilerParams(dimension_semantics=("arbitrary",)),
        name="moe_dispatch",
    )(_tile_dest(plan, t, tm), x, jnp.zeros((n, D_MODEL), x.dtype))


def _grouped_ffn_kernel(te_ref, nv_ref, x_ref, wg_ref, wu_ref, wd_ref, o_ref, xb_sc, acc_sc):
    i, f = pl.program_id(0), pl.program_id(1)
    last = f == pl.num_programs(1) - 1
    valid = i < nv_ref[0]

    @pl.when(valid)
    def _():
        @pl.when(f == 0)
        def _():
            xb_sc[...] = x_ref[...].astype(BF16)
            acc_sc[...] = jnp.zeros_like(acc_sc)

        acc_sc[...] += _swiglu_block(xb_sc[...], wg_ref[0], wu_ref[0], wd_ref[0])

        @pl.when(last)
        def _():
            o_ref[...] = acc_sc[...]

    @pl.when(jnp.logical_and(jnp.logical_not(valid), last))
    def _():
        o_ref[...] = jnp.zeros_like(o_ref)


def _grouped_ffn(plan, xs, wg, wu, wd, tm):
    tf = D_FF
    row = pl.BlockSpec((tm, D_MODEL), lambda i, f, te, nv: (i, 0))
    once = pl.Buffered(1)
    return pl.pallas_call(
        _grouped_ffn_kernel,
        grid_spec=pltpu.PrefetchScalarGridSpec(
            num_scalar_prefetch=2,
            grid=(plan['n_tiles'], D_FF // tf),
            in_specs=[row,
                      pl.BlockSpec((1, D_MODEL, tf), lambda i, f, te, nv: (te[i], 0, f),
                                   pipeline_mode=once),
                      pl.BlockSpec((1, D_MODEL, tf), lambda i, f, te, nv: (te[i], 0, f),
                                   pipeline_mode=once),
                      pl.BlockSpec((1, tf, D_MODEL), lambda i, f, te, nv: (te[i], f, 0),
                                   pipeline_mode=once)],
            out_specs=row,
            scratch_shapes=[pltpu.VMEM((tm, D_MODEL), BF16),
                            pltpu.VMEM((tm, D_MODEL), F32)]),
        out_shape=jax.ShapeDtypeStruct(xs.shape, F32),
        compiler_params=pltpu.CompilerParams(
            dimension_semantics=("parallel", "arbitrary"), vmem_limit_bytes=VMEM_LIMIT),
        name="moe_grouped_ffn",
    )(plan['tile_expert'], plan['n_valid'], xs, wg, wu, wd)


def _combine_kernel(dest_ref, x_ref, w_ref, ys_hbm, g_ref, b_ref, o_ref, ybuf, sem):
    tm = x_ref.shape[0]

    def issue(j, c):
        _row_copy(ys_hbm, dest_ref[0, 0, j], ybuf, j, sem).start()
        return c

    lax.fori_loop(0, 2 * tm, issue, 0, unroll=DMA_UNROLL)
    pltpu.make_async_copy(ys_hbm.at[pl.ds(0, 2 * tm)], ybuf, sem.at[0]).wait()
    w = w_ref[...]
    y = w[:, 0:1] * ybuf[0:tm, :] + w[:, 1:2] * ybuf[tm:2 * tm, :]
    o_ref[...] = _layernorm(ALPHA * x_ref[...] + y, g_ref[...], b_ref[...])


def _combine(plan, x, rw, ys, g, b):
    t = x.shape[0]
    tm = COMBINE_TM
    dest = _tile_dest(plan, t, tm)
    full = lambda a: pl.BlockSpec(a.shape, lambda i: (0, 0))
    return pl.pallas_call(
        _combine_kernel,
        grid=(t // tm,),
        in_specs=[pl.BlockSpec((1, 1, 2 * tm), lambda i: (i, 0, 0), memory_space=pltpu.SMEM),
                  pl.BlockSpec((tm, D_MODEL), lambda i: (i, 0)),
                  pl.BlockSpec((tm, LANES), lambda i: (i, 0)),
                  pl.BlockSpec(memory_space=pl.ANY), full(g), full(b)],
        out_specs=pl.BlockSpec((tm, D_MODEL), lambda i: (i, 0)),
        out_shape=jax.ShapeDtypeStruct((t, D_MODEL), F32),
        scratch_shapes=[pltpu.VMEM((2 * tm, D_MODEL), F32), pltpu.SemaphoreType.DMA((1,))],
        compiler_params=pltpu.CompilerParams(dimension_semantics=("arbitrary",)),
        name="moe_combine_ln",
    )(dest, x, rw, ys, g, b)


def kernel(x_prompt, x_sample, w_in, mla_q_norm, mla_w_qb, mla_kv_norm, mla_w_kvb,
           gqa_q_norm, gqa_k_norm, diff_lambda, diff_out_norm, w_out, ln1_g, ln1_b,
           ffn_w_gate, ffn_w_up, ffn_w_down, moe_router, moe_w_gate, moe_w_up,
           moe_w_down, ln2_g, ln2_b):
    tables = _rope_tables()
    layers = [_layer_weights(l, w_in, mla_q_norm, mla_w_qb, mla_kv_norm, mla_w_kvb,
                             gqa_q_norm, gqa_k_norm, w_out) for l in range(DEPTH)]
    ffn_w = (ffn_w_gate.astype(BF16), ffn_w_up.astype(BF16), ffn_w_down.astype(BF16))
    moe_w = (moe_w_gate.astype(BF16), moe_w_up.astype(BF16), moe_w_down.astype(BF16))
    routers = []
    for m in range(moe_router.shape[0]):
        wr = jnp.pad(moe_router[m], ((0, 0), (0, LANES - N_EXPERTS)))
        wr_hi = wr.astype(BF16)
        routers.append((wr_hi, (wr - wr_hi.astype(F32)).astype(BF16)))

    def run(x):
        b, s, _ = x.shape
        t = b * s
        xf = x.reshape(t, D_MODEL)
        for l in range(DEPTH):
            lw = layers[l]
            mq, mk, mvt, gq, gk, gvt, dq, dk, dvt = _prep(xf, tables, lw, s)
            r3 = lambda a: a.reshape(b, s, a.shape[-1])
            o_m = _attention(r3(mq), r3(mk), mvt, mode='mla')
            o_g = _attention(r3(gq), r3(gk), gvt, mode='gqa')
            lam_init = 0.8 - 0.6 * math.exp(-0.3 * l)
            o_d = _attention(r3(dq), r3(dk), dvt, mode='diff',
                             lam_params=diff_lambda[l].astype(F32),
                             gain=jnp.tile(diff_out_norm[l], 2).reshape(1, LANES),
                             lam_init=lam_init)
            mix = (o_m.reshape(t, -1), o_g.reshape(t, -1), o_d.reshape(t, -1))
            g1, b1 = ln1_g[l].reshape(1, -1), ln1_b[l].reshape(1, -1)
            g2, b2 = ln2_g[l].reshape(1, -1), ln2_b[l].reshape(1, -1)
            m = l // 2
            if l % 2 == 0:
                x1 = _outproj(xf, *mix, lw, g1, b1)
                xf = _ffn(x1, ffn_w[0][m], ffn_w[1][m], ffn_w[2][m], g2, b2)
            else:
                x1, ri, rw = _outproj(xf, *mix, lw, g1, b1, router=routers[m])
                plan = _route_plan(ri, FFN_TM)
                xs = _dispatch(plan, x1, FFN_TM)
                ys = _grouped_ffn(plan, xs, moe_w[0][m], moe_w[1][m], moe_w[2][m], FFN_TM)
                xf = _combine(plan, x1, rw, ys, g2, b2)
        return xf.reshape(b, s, D_MODEL)

    return (run(x_prompt), run(x_sample))
```

```python
import functools
import math

import jax
import jax.numpy as jnp
from jax import lax
from jax.experimental import pallas as pl
from jax.experimental.pallas import tpu as pltpu

D_MODEL = 1024
DEPTH = 2
GRID_W = 64
ROPE_THETA = 10000.0
MLA_HEADS = 6
MLA_Q_RANK = 256
MLA_KV_RANK = 128
MLA_NOPE = 64
MLA_ROPE = 32
MLA_V = 64
GQA_HEADS = 6
GQA_KV_HEADS = 2
GQA_DIM = 64
DIFF_HEADS = 4
DIFF_QK = 32
DIFF_V = 64
D_FF = 2816
N_EXPERTS = 8
ALPHA = (2 * DEPTH) ** 0.25
LN_EPS = 1e-5
RMS_EPS = 1e-6

LOG2E = math.log2(math.e)
MLA_SCALE = (MLA_NOPE + MLA_ROPE) ** -0.5 * LOG2E
GQA_SCALE = GQA_DIM ** -0.5 * LOG2E
DIFF_SCALE = DIFF_QK ** -0.5 * LOG2E

LANES = 128
VMEM_LIMIT = 56 * 1024 * 1024
MAX_SEQ = 4096

PREP_TM = 512
ATTN_SCRATCH_BYTES = 32 * 1024 * 1024
ATTN_SB_ELEMS = 16 * 1024
ATTN_RB = 1024
PIPE_LAG = 2
ONES_ROWS = 16
VT_ROWS = LANES + ONES_ROWS
PROJ_TM = 1024
FFN_TM = 512
FFN_TF = 1408
COMBINE_TM = 256
DMA_UNROLL = 16

BF16 = jnp.bfloat16
F32 = jnp.float32

_SEG = dict(cq=(0, 256), ckv=(256, 384), kr=(384, 512), krr=(512, 640),
            gq=(640, 1024), gqr=(1024, 1408), gk=(1408, 1536), gkr=(1536, 1664),
            gv=(1664, 1792), dq=(1792, 2048), dqr=(2048, 2304), dk=(2304, 2560),
            dkr=(2560, 2816), dv=(2816, 3072))
PREP_N = 3072
GQA_ORDER = (0, 3, 1, 4, 2, 5)


def _rot_cols(w, half=16):
    k, n = w.shape
    w4 = w.reshape(k, n // (2 * half), 2, half)
    return jnp.stack([-w4[:, :, 1, :], w4[:, :, 0, :]], axis=2).reshape(k, n)


def _rope_tables():
    pos = jnp.arange(MAX_SEQ, dtype=F32)
    inv = ROPE_THETA ** (-jnp.arange(0, 32, 2, dtype=F32) / 32)

    def cs(p):
        ang = p[:, None] * inv[None, :]
        c, s = jnp.cos(ang), jnp.sin(ang)
        return jnp.concatenate([c, c], -1), jnp.concatenate([s, s], -1)

    c32, s32 = cs(pos)
    row = jnp.floor(pos / GRID_W)
    col = pos - row * GRID_W
    cr, sr = cs(row)
    cc, sc = cs(col)
    ones = jnp.ones((MAX_SEQ, 64), F32)
    z64 = jnp.zeros((MAX_SEQ, 64), F32)
    z32 = jnp.zeros((MAX_SEQ, 32), F32)
    a_c = jnp.concatenate([ones, c32, z32], -1)
    a_s = jnp.concatenate([z64, s32, z32], -1)
    p_c = jnp.tile(c32, (1, 4))
    p_s = jnp.tile(s32, (1, 4))
    x_c = jnp.tile(jnp.concatenate([cr, cc], -1), (1, 2))
    x_s = jnp.tile(jnp.concatenate([sr, sc], -1), (1, 2))
    return a_c, a_s, p_c, p_s, x_c, x_s


def _layer_weights(l, w_in, mla_q_norm, mla_w_qb, mla_kv_norm, mla_w_kvb, gqa_q_norm,
                   gqa_k_norm, w_out):
    w = w_in[l]
    o = [0, 256, 384, 416, 800, 928, 1056, 1312, 1568, 1824]
    cq, ckv, kr, gq, gk, gv, dq, dk, dv = [w[:, o[i]:o[i + 1]] for i in range(9)]
    z = lambda n: jnp.zeros((D_MODEL, n), F32)
    kr_p = jnp.concatenate([z(64), kr, z(32)], -1)
    krr_p = jnp.concatenate([z(64), _rot_cols(kr), z(32)], -1)
    order = jnp.asarray(GQA_ORDER)
    gq_p = gq.reshape(D_MODEL, GQA_HEADS, GQA_DIM)[:, order, :].reshape(D_MODEL, -1)
    w_all = jnp.concatenate(
        [cq, ckv, kr_p, krr_p, gq_p, _rot_cols(gq_p), gk, _rot_cols(gk), gv,
         dq, _rot_cols(dq), dk, _rot_cols(dk), dv], axis=-1).astype(BF16)

    qb = mla_w_qb[l].reshape(MLA_Q_RANK, MLA_HEADS, MLA_NOPE + MLA_ROPE)
    qb_nope, qb_rope = qb[:, :, :MLA_NOPE], qb[:, :, MLA_NOPE:]
    qb_rope_rot = _rot_cols(qb_rope.reshape(MLA_Q_RANK, -1)).reshape(qb_rope.shape)
    zq = lambda n: jnp.zeros((MLA_Q_RANK, MLA_HEADS, n), F32)
    w_qb = jnp.concatenate(
        [jnp.concatenate([qb_nope, qb_rope, zq(32)], -1).reshape(MLA_Q_RANK, -1),
         jnp.concatenate([zq(64), qb_rope_rot, zq(32)], -1).reshape(MLA_Q_RANK, -1)],
        axis=-1).astype(BF16)

    kvb = mla_w_kvb[l].reshape(MLA_KV_RANK, MLA_HEADS, MLA_NOPE + MLA_V)
    w_kn = jnp.concatenate(
        [kvb[:, :, :MLA_NOPE], jnp.zeros((MLA_KV_RANK, MLA_HEADS, 64), F32)],
        -1).reshape(MLA_KV_RANK, -1).astype(BF16)
    w_v = kvb[:, :, MLA_NOPE:].reshape(MLA_KV_RANK, -1).astype(BF16)

    def swap16(g):
        return g.reshape(2, 2, 16)[:, ::-1, :].reshape(64)

    gqn, gkn = gqa_q_norm[l], gqa_k_norm[l]
    vecs = jnp.zeros((8, 256), F32)
    vecs = vecs.at[0, :].set(mla_q_norm[l])
    vecs = vecs.at[1, :128].set(mla_kv_norm[l])
    vecs = vecs.at[2, :128].set(jnp.tile(gqn, 2))
    vecs = vecs.at[3, :128].set(jnp.tile(swap16(gqn), 2))
    vecs = vecs.at[4, :128].set(jnp.tile(gkn, 2))
    vecs = vecs.at[5, :128].set(jnp.tile(swap16(gkn), 2))

    wo = w_out[l]
    wo_g = wo[384:768].reshape(GQA_HEADS, GQA_DIM, D_MODEL)[order, :, :].reshape(384, D_MODEL)
    wo = jnp.concatenate([wo[:384], wo_g, wo[768:]], axis=0).astype(BF16)
    return dict(w_all=w_all, w_qb=w_qb, w_kn=w_kn, w_v=w_v, vecs=vecs, wo=wo)


def _store_transposed(vt_ref, v):
    tm = v.shape[0]
    for p in range(v.shape[1] // LANES):
        vt_ref[p * VT_ROWS:p * VT_ROWS + LANES, :] = (
            v[:, p * LANES:(p + 1) * LANES].T.astype(BF16))
        vt_ref[p * VT_ROWS + LANES:(p + 1) * VT_ROWS, :] = jnp.ones((ONES_ROWS, tm), BF16)


def _prep_kernel(x_ref, ac_ref, as_ref, pc_ref, ps_ref, xc_ref, xs_ref,
                 wall_ref, wqb_ref, wkn_ref, wv_ref, vec_ref,
                 mq_ref, mk_ref, mvt_ref, gq_ref, gk_ref, gvt_ref, dq_ref, dk_ref, dvt_ref):
    tm = x_ref.shape[0]
    xb = x_ref[...].astype(BF16)
    h = jnp.dot(xb, wall_ref[...], preferred_element_type=F32)
    seg = lambda name: h[:, _SEG[name][0]:_SEG[name][1]]
    a_c, a_s = ac_ref[...], as_ref[...]
    p_c, p_s = pc_ref[...], ps_ref[...]
    x_c, x_s = xc_ref[...], xs_ref[...]

    cq = seg('cq')
    cqn = cq * lax.rsqrt(jnp.mean(cq * cq, axis=-1, keepdims=True) + RMS_EPS)
    cqb = (cqn * vec_ref[0:1, :]).astype(BF16)
    q2 = jnp.dot(cqb, wqb_ref[...], preferred_element_type=F32)
    ckv = seg('ckv')
    ckvn = ckv * lax.rsqrt(jnp.mean(ckv * ckv, axis=-1, keepdims=True) + RMS_EPS)
    ckvb = (ckvn * vec_ref[1:2, 0:128]).astype(BF16)
    kn = jnp.dot(ckvb, wkn_ref[...], preferred_element_type=F32)
    kro = seg('kr') * a_c + seg('krr') * a_s
    for hd in range(MLA_HEADS):
        sl = slice(hd * LANES, (hd + 1) * LANES)
        sr = slice(768 + hd * LANES, 768 + (hd + 1) * LANES)
        mq_ref[:, sl] = ((q2[:, sl] * a_c + q2[:, sr] * a_s) * MLA_SCALE).astype(BF16)
        mk_ref[:, sl] = (kn[:, sl] + kro).astype(BF16)
    _store_transposed(mvt_ref, jnp.dot(ckvb, wv_ref[...], preferred_element_type=F32))

    lane = lax.broadcasted_iota(jnp.int32, (tm, LANES), 1)
    lo = lane < 64

    def half_rms(z):
        zz = z * z
        s_lo = jnp.sum(jnp.where(lo, zz, 0.0), axis=-1, keepdims=True)
        s_hi = jnp.sum(jnp.where(lo, 0.0, zz), axis=-1, keepdims=True)
        return jnp.where(lo, lax.rsqrt(s_lo * (1.0 / 64) + RMS_EPS),
                         lax.rsqrt(s_hi * (1.0 / 64) + RMS_EPS))

    gq, gqr = seg('gq'), seg('gqr')
    g_q, g_qs = vec_ref[2:3, 0:128], vec_ref[3:4, 0:128]
    for j in range(3):
        sl = slice(j * LANES, (j + 1) * LANES)
        z, zr = gq[:, sl], gqr[:, sl]
        r = half_rms(z)
        gq_ref[:, sl] = (r * ((z * g_q) * x_c + (zr * g_qs) * x_s) * GQA_SCALE).astype(BF16)
    z, zr = seg('gk'), seg('gkr')
    g_k, g_ks = vec_ref[4:5, 0:128], vec_ref[5:6, 0:128]
    gk_ref[...] = (half_rms(z) * ((z * g_k) * x_c + (zr * g_ks) * x_s)).astype(BF16)
    _store_transposed(gvt_ref, seg('gv'))

    dq, dqr, dk, dkr = seg('dq'), seg('dqr'), seg('dk'), seg('dkr')
    for j in range(2):
        sl = slice(j * LANES, (j + 1) * LANES)
        dq_ref[:, sl] = ((dq[:, sl] * p_c + dqr[:, sl] * p_s) * DIFF_SCALE).astype(BF16)
        dk_ref[:, sl] = (dk[:, sl] * p_c + dkr[:, sl] * p_s).astype(BF16)
    _store_transposed(dvt_ref, seg('dv'))


def _prep(x, tables, lw, seq):
    t = x.shape[0]
    tm = PREP_TM
    nblk = seq // tm
    row = lambda w: pl.BlockSpec((tm, w), lambda i: (i, 0))
    tab = pl.BlockSpec((tm, LANES), lambda i: (i % nblk, 0))
    full = lambda a: pl.BlockSpec(a.shape, lambda i: (0, 0), pipeline_mode=pl.Buffered(1))
    col = lambda pairs: pl.BlockSpec((pairs * VT_ROWS, tm), lambda i: (0, i))
    outs = ((768, row), (768, row), (3, col), (384, row), (128, row), (1, col),
            (256, row), (256, row), (2, col))
    return pl.pallas_call(
        _prep_kernel,
        grid=(t // tm,),
        in_specs=[row(D_MODEL)] + [tab] * 6
        + [full(lw['w_all']), full(lw['w_qb']), full(lw['w_kn']), full(lw['w_v']),
           full(lw['vecs'])],
        out_specs=[spec(w) for w, spec in outs],
        out_shape=[jax.ShapeDtypeStruct((w * VT_ROWS, t) if spec is col else (t, w), BF16)
                   for w, spec in outs],
        compiler_params=pltpu.CompilerParams(
            dimension_semantics=("parallel",), vmem_limit_bytes=VMEM_LIMIT),
        name="mixer_prep",
    )(x, *tables, lw['w_all'], lw['w_qb'], lw['w_kn'], lw['w_v'], lw['vecs'])


_NT = (((1,), (1,)), ((), ()))


def _normalised(oe):
    return oe[:LANES, :] * (1.0 / oe[LANES:LANES + 1, :])


def _lane_mask(q, keep):
    return jnp.where(keep, q.astype(F32), 0.0).astype(BF16)


def _stage_scores_numerators(q_ref, k_ref, st, m_sc, e_old, mode, head):
    q = q_ref[0]
    tq = q.shape[0]
    n_keys = k_ref.shape[1]
    lane = lax.broadcasted_iota(jnp.int32, (tq, LANES), 1)
    lo_l, hi_l = slice(0, LANES), slice(LANES, 2 * LANES)
    if mode == 'mla':
        ops = ((lo_l, q[:, :LANES]), (hi_l, q[:, LANES:]))
    elif mode == 'gqa':
        ops = ((lo_l, _lane_mask(q, lane < 64)), (lo_l, _lane_mask(q, lane >= 64)))
    else:
        lo = 64 * head
        ops = tuple((lo_l, _lane_mask(q, (lane >= lo + 32 * j) & (lane < lo + 32 * j + 32)))
                    for j in range(2))
    sb = ATTN_SB_ELEMS // tq
    for j, (k_lanes, q_op) in enumerate(ops):
        m_run = None
        m_o = m_sc[j:j + 1, :]
        for c in range(n_keys // ATTN_RB):
            rows = slice(c * ATTN_RB, (c + 1) * ATTN_RB)
            for r in range(ATTN_RB // sb):
                rr = slice(c * ATTN_RB + r * sb, c * ATTN_RB + (r + 1) * sb)
                e_old[j][rr, :] = jnp.exp2(st[j][rr, :] - m_o).astype(BF16)
            s_c = lax.dot_general(k_ref[0, rows, k_lanes], q_op, _NT,
                                  preferred_element_type=F32)
            st[j][rows, :] = s_c
            cm = jnp.max(s_c, axis=0, keepdims=True)
            m_run = cm if m_run is None else jnp.maximum(m_run, cm)
        m_sc[j:j + 1, :] = m_run


def _stage_pv(vt_ref, e, o_ref, mode, head, lam_ref, gain_ref, lam_init):
    vt = vt_ref[...]
    pv = [_normalised(jnp.dot(vt, e[j][...], preferred_element_type=F32)) for j in range(2)]
    row = lax.broadcasted_iota(jnp.int32, pv[0].shape, 0)
    if mode != 'diff':
        o_ref[0] = jnp.where(row < 64, pv[0], pv[1]).T.astype(o_ref.dtype)
        return
    lp = lam_ref[...]
    lam = (jnp.exp(jnp.sum(lp[0:1, :] * lp[1:2, :], axis=-1, keepdims=True))
           - jnp.exp(jnp.sum(lp[2:3, :] * lp[3:4, :], axis=-1, keepdims=True)) + lam_init)
    mine = (row >= 64 * head) & (row < 64 * head + 64)
    o = jnp.where(mine, pv[0] - lam * pv[1], 0.0)
    ms = jnp.sum(o * o, axis=0, keepdims=True) * (1.0 / DIFF_V)
    new = (o * lax.rsqrt(ms + RMS_EPS)).T * gain_ref[...] * (1.0 - lam_init)
    if head == 0:
        o_ref[0] = new.astype(o_ref.dtype)
    else:
        lane = lax.broadcasted_iota(jnp.int32, new.shape, 1)
        o_ref[0] = jnp.where(lane >= 64, new, o_ref[0].astype(F32)).astype(o_ref.dtype)


def _attn_kernel(*refs, mode, lam_init):
    if mode == 'diff':
        lam_ref, gain_ref, refs = refs[0], refs[1], refs[2:]
    else:
        lam_ref = gain_ref = None
    q_ref, k_ref, vt_ref, o_ref, st0, st1, e00, e01, e10, e11, m_sc = refs
    step = pl.program_id(0)
    st, e = (st0, st1), ((e00, e01), (e10, e11))

    @pl.when(step == 0)
    def _():
        for buf in (st0, st1, m_sc):
            buf[...] = jnp.zeros_like(buf)
        for buf in (e00, e01, e10, e11):
            buf[...] = jnp.ones_like(buf)

    for par in range(2):
        @pl.when(step % 2 == par)
        def _():
            _stage_pv(vt_ref, e[par], o_ref, mode, par, lam_ref, gain_ref, lam_init)
            _stage_scores_numerators(q_ref, k_ref, st, m_sc, e[1 - par], mode, par)


def _attention(q, k, vt, *, mode, lam_params=None, gain=None, lam_init=None):
    b, s, _ = q.shape
    tq = ATTN_SCRATCH_BYTES // ((2 * 4 + 4 * 2) * s)
    if mode == 'mla':
        npairs, qw, kw, per_tile = 3, 2 * LANES, 2 * LANES, 1
        kmap = lambda bi, p, i: (bi, 0, p)
        vmap = lambda bi, p, i: (p, bi)
    elif mode == 'gqa':
        npairs, qw, kw, per_tile = 3, LANES, LANES, 1
        kmap = lambda bi, p, i: (bi, 0, 0)
        vmap = lambda bi, p, i: (0, bi)
    else:
        npairs, qw, kw, per_tile = 2, LANES, LANES, 2
        kmap = lambda bi, p, i: (bi, 0, p)
        vmap = lambda bi, p, i: (p, bi)
    qmap = lambda bi, p, i: (bi, i, p)
    nq = s // tq
    n_units = b * npairs * nq * per_tile

    def lagged(index_map, lag):
        def wrapped(step):
            tile = jnp.clip(step - lag, 0, n_units - 1) // per_tile
            return index_map(tile // (nq * npairs), (tile // nq) % npairs, tile % nq)
        return wrapped

    in_specs = [pl.BlockSpec((1, tq, qw), lagged(qmap, 0)),
                pl.BlockSpec((1, s, kw), lagged(kmap, 0)),
                pl.BlockSpec((VT_ROWS, s), lagged(vmap, PIPE_LAG))]
    args = [q, k, vt]
    if mode == 'diff':
        in_specs = [pl.BlockSpec(lam_params.shape, lambda step: (0, 0)),
                    pl.BlockSpec(gain.shape, lambda step: (0, 0))] + in_specs
        args = [lam_params, gain] + args
    return pl.pallas_call(
        functools.partial(_attn_kernel, mode=mode, lam_init=lam_init),
        grid=(n_units + PIPE_LAG,),
        in_specs=in_specs,
        out_specs=pl.BlockSpec((1, tq, LANES), lagged(qmap, PIPE_LAG)),
        out_shape=jax.ShapeDtypeStruct((b, s, npairs * LANES), BF16),
        scratch_shapes=([pltpu.VMEM((s, tq), F32)] * 2 + [pltpu.VMEM((s, tq), BF16)] * 4
                        + [pltpu.VMEM((8, tq), F32)]),
        compiler_params=pltpu.CompilerParams(
            dimension_semantics=("arbitrary",), vmem_limit_bytes=VMEM_LIMIT),
        name="attn_" + mode,
    )(*args)


def _layernorm(z, g, b):
    mu = jnp.mean(z, axis=-1, keepdims=True)
    zc = z - mu
    var = jnp.mean(zc * zc, axis=-1, keepdims=True)
    return zc * lax.rsqrt(var + LN_EPS) * g + b


def _router_top2(x, wr_hi, wr_lo):
    x_hi = x.astype(BF16)
    x_lo = (x - x_hi.astype(F32)).astype(BF16)
    logits = (jnp.dot(x_hi, wr_hi, preferred_element_type=F32)
              + jnp.dot(x_lo, wr_hi, preferred_element_type=F32)
              + jnp.dot(x_hi, wr_lo, preferred_element_type=F32))
    l8 = logits.T[0:N_EXPERTS, :]
    row = lax.broadcasted_iota(jnp.int32, l8.shape, 0)
    neg = -jnp.inf
    m1 = jnp.max(l8, axis=0, keepdims=True)
    i1 = jnp.min(jnp.where(l8 == m1, row, N_EXPERTS), axis=0, keepdims=True)
    rest = jnp.where(row == i1, neg, l8)
    m2 = jnp.max(rest, axis=0, keepdims=True)
    i2 = jnp.min(jnp.where(rest == m2, row, N_EXPERTS), axis=0, keepdims=True)
    e2 = jnp.exp(m2 - m1)
    w1 = 1.0 / (1.0 + e2)
    ri = jnp.where(row == 0, i1, jnp.where(row == 1, i2, 0))
    rw = jnp.where(row == 0, w1, jnp.where(row == 1, e2 * w1, 0.0))
    pad = lambda a: jnp.concatenate(
        [a, jnp.zeros((LANES - N_EXPERTS, a.shape[1]), a.dtype)], axis=0).T
    return pad(ri), pad(rw)


def _outproj_kernel(*refs, route):
    if route:
        (x_ref, om_ref, og_ref, od_ref, w_ref, g_ref, b_ref,
         rhi_ref, rlo_ref, o_ref, ri_ref, rw_ref) = refs
    else:
        x_ref, om_ref, og_ref, od_ref, w_ref, g_ref, b_ref, o_ref = refs
    mix = jnp.concatenate([om_ref[...], og_ref[...], od_ref[...]], axis=-1)
    y = jnp.dot(mix, w_ref[...], preferred_element_type=F32)
    x1 = _layernorm(ALPHA * x_ref[...] + y, g_ref[...], b_ref[...])
    o_ref[...] = x1
    if route:
        ri_ref[...], rw_ref[...] = _router_top2(x1, rhi_ref[...], rlo_ref[...])


def _outproj(x, o_m, o_g, o_d, lw, g, b, router=None):
    t = x.shape[0]
    tm = PROJ_TM
    route = router is not None
    row = lambda w: pl.BlockSpec((tm, w), lambda i: (i, 0))
    full = lambda a: pl.BlockSpec(a.shape, lambda i: (0, 0))
    in_specs = [row(D_MODEL), row(384), row(384), row(256), full(lw['wo']), full(g), full(b)]
    args = [x, o_m, o_g, o_d, lw['wo'], g, b]
    out_specs = [row(D_MODEL)]
    out_shape = [jax.ShapeDtypeStruct((t, D_MODEL), F32)]
    if route:
        in_specs += [full(router[0]), full(router[1])]
        args += list(router)
        out_specs += [row(LANES), row(LANES)]
        out_shape += [jax.ShapeDtypeStruct((t, LANES), jnp.int32),
                      jax.ShapeDtypeStruct((t, LANES), F32)]
    outs = pl.pallas_call(
        functools.partial(_outproj_kernel, route=route),
        grid=(t // tm,),
        in_specs=in_specs,
        out_specs=out_specs,
        out_shape=out_shape,
        compiler_params=pltpu.CompilerParams(
            dimension_semantics=("parallel",), vmem_limit_bytes=VMEM_LIMIT),
        name="outproj_ln_route" if route else "outproj_ln",
    )(*args)
    return outs if route else outs[0]


def _swiglu_block(xb, wg, wu, wd):
    gt = jnp.dot(xb, wg, preferred_element_type=F32)
    up = jnp.dot(xb, wu, preferred_element_type=F32)
    act = (gt / (1.0 + jnp.exp(-gt)) * up).astype(BF16)
    return jnp.dot(act, wd, preferred_element_type=F32)


def _ffn_kernel(x_ref, wg_ref, wu_ref, wd_ref, g_ref, b_ref, o_ref, xb_sc, acc_sc):
    f = pl.program_id(1)

    @pl.when(f == 0)
    def _():
        xb_sc[...] = x_ref[...].astype(BF16)
        acc_sc[...] = jnp.zeros_like(acc_sc)

    acc_sc[...] += _swiglu_block(xb_sc[...], wg_ref[...], wu_ref[...], wd_ref[...])

    @pl.when(f == pl.num_programs(1) - 1)
    def _():
        o_ref[...] = _layernorm(ALPHA * x_ref[...] + acc_sc[...], g_ref[...], b_ref[...])


def _ffn(x, wg, wu, wd, g, b):
    t = x.shape[0]
    tm, tf = FFN_TM, D_FF
    row = pl.BlockSpec((tm, D_MODEL), lambda i, f: (i, 0))
    full = lambda a: pl.BlockSpec(a.shape, lambda i, f: (0, 0))
    once = pl.Buffered(1)
    return pl.pallas_call(
        _ffn_kernel,
        grid=(t // tm, D_FF // tf),
        in_specs=[row,
                  pl.BlockSpec((D_MODEL, tf), lambda i, f: (0, f), pipeline_mode=once),
                  pl.BlockSpec((D_MODEL, tf), lambda i, f: (0, f), pipeline_mode=once),
                  pl.BlockSpec((tf, D_MODEL), lambda i, f: (f, 0), pipeline_mode=once),
                  full(g), full(b)],
        out_specs=row,
        out_shape=jax.ShapeDtypeStruct((t, D_MODEL), F32),
        scratch_shapes=[pltpu.VMEM((tm, D_MODEL), BF16), pltpu.VMEM((tm, D_MODEL), F32)],
        compiler_params=pltpu.CompilerParams(
            dimension_semantics=("parallel", "arbitrary"), vmem_limit_bytes=VMEM_LIMIT),
        name="ffn_ln",
    )(x, wg, wu, wd, g, b)


def _route_plan(ri, tm):
    ea = ri[:, :2].reshape(-1)
    n_assign = ea.shape[0]
    experts = jnp.arange(N_EXPERTS, dtype=jnp.int32)
    oh = (ea[:, None] == experts[None, :]).astype(jnp.int32)
    csum = jnp.cumsum(oh, axis=0)
    rank = jnp.sum((csum - oh) * oh, axis=1)
    tiles = (csum[-1] + tm - 1) // tm
    tile_end = jnp.cumsum(tiles)
    start = (tile_end - tiles) * tm
    dest = jnp.sum(oh * start[None, :], axis=1) + rank
    n_tiles = n_assign // tm + N_EXPERTS
    tile_ids = jnp.arange(n_tiles, dtype=jnp.int32)
    tile_expert = jnp.sum((tile_ids[:, None] >= tile_end[None, :]).astype(jnp.int32), axis=1)
    last_used = jnp.max(jnp.where(tiles > 0, experts, 0))
    tile_expert = jnp.minimum(tile_expert, last_used).astype(jnp.int32)
    return dict(dest=dest.astype(jnp.int32), tile_expert=tile_expert,
                n_valid=tile_end[-1:].astype(jnp.int32), n_tiles=n_tiles)


def _tile_dest(plan, t, tm):
    return plan['dest'].reshape(t // tm, tm, 2).transpose(0, 2, 1).reshape(t // tm, 1, 2 * tm)


def _row_copy(src, src_row, dst, dst_row, sem):
    return pltpu.make_async_copy(src.at[pl.ds(src_row, 1)], dst.at[pl.ds(dst_row, 1)],
                                 sem.at[0])


def _dispatch_kernel(dest_ref, x_ref, init_hbm, o_hbm, sem):
    del init_hbm
    tm = x_ref.shape[0]

    def issue(g, c):
        for u in range(DMA_UNROLL // 2):
            j = g * (DMA_UNROLL // 2) + u
            _row_copy(x_ref, j, o_hbm, dest_ref[0, 0, j], sem).start(priority=0)
            _row_copy(x_ref, j, o_hbm, dest_ref[0, 0, tm + j], sem).start(priority=1)
        return c

    lax.fori_loop(0, tm // (DMA_UNROLL // 2), issue, 0)
    for _ in range(2):
        pltpu.make_async_copy(x_ref, o_hbm.at[pl.ds(0, tm)], sem.at[0]).wait()


def _dispatch(plan, x, row_tile):
    t = x.shape[0]
    tm = COMBINE_TM
    n = plan['n_tiles'] * row_tile
    return pl.pallas_call(
        _dispatch_kernel,
        grid=(t // tm,),
        in_specs=[pl.BlockSpec((1, 1, 2 * tm), lambda i: (i, 0, 0), memory_space=pltpu.SMEM),
                  pl.BlockSpec((tm, D_MODEL), lambda i: (i, 0)),
                  pl.BlockSpec(memory_space=pl.ANY)],
        out_specs=pl.BlockSpec(memory_space=pl.ANY),
        out_shape=jax.ShapeDtypeStruct((n, D_MODEL), x.dtype),
        input_output_aliases={2: 0},
        scratch_shapes=[pltpu.SemaphoreType.DMA((1,))],
        compiler_params=pltpu.CompilerParams(dimension_semantics=("arbitrary",)),
        name="moe_dispatch",
    )(_tile_dest(plan, t, tm), x, jnp.zeros((n, D_MODEL), x.dtype))


def _grouped_ffn_kernel(te_ref, nv_ref, x_ref, wg_ref, wu_ref, wd_ref, o_ref, xb_sc, acc_sc):
    i, f = pl.program_id(0), pl.program_id(1)
    last = f == pl.num_programs(1) - 1
    valid = i < nv_ref[0]

    @pl.when(valid)
    def _():
        @pl.when(f == 0)
        def _():
            xb_sc[...] = x_ref[...].astype(BF16)
            acc_sc[...] = jnp.zeros_like(acc_sc)

        acc_sc[...] += _swiglu_block(xb_sc[...], wg_ref[0], wu_ref[0], wd_ref[0])

        @pl.when(last)
        def _():
            o_ref[...] = acc_sc[...]

    @pl.when(jnp.logical_and(jnp.logical_not(valid), last))
    def _():
        o_ref[...] = jnp.zeros_like(o_ref)


def _grouped_ffn(plan, xs, wg, wu, wd, tm):
    tf = D_FF
    row = pl.BlockSpec((tm, D_MODEL), lambda i, f, te, nv: (i, 0))
    once = pl.Buffered(1)
    return pl.pallas_call(
        _grouped_ffn_kernel,
        grid_spec=pltpu.PrefetchScalarGridSpec(
            num_scalar_prefetch=2,
            grid=(plan['n_tiles'], D_FF // tf),
            in_specs=[row,
                      pl.BlockSpec((1, D_MODEL, tf), lambda i, f, te, nv: (te[i], 0, f),
                                   pipeline_mode=once),
                      pl.BlockSpec((1, D_MODEL, tf), lambda i, f, te, nv: (te[i], 0, f),
                                   pipeline_mode=once),
                      pl.BlockSpec((1, tf, D_MODEL), lambda i, f, te, nv: (te[i], f, 0),
                                   pipeline_mode=once)],
            out_specs=row,
            scratch_shapes=[pltpu.VMEM((tm, D_MODEL), BF16),
                            pltpu.VMEM((tm, D_MODEL), F32)]),
        out_shape=jax.ShapeDtypeStruct(xs.shape, F32),
        compiler_params=pltpu.CompilerParams(
            dimension_semantics=("parallel", "arbitrary"), vmem_limit_bytes=VMEM_LIMIT),
        name="moe_grouped_ffn",
    )(plan['tile_expert'], plan['n_valid'], xs, wg, wu, wd)


def _combine_kernel(dest_ref, x_ref, w_ref, ys_hbm, g_ref, b_ref, o_ref, ybuf, sem):
    tm = x_ref.shape[0]

    def issue(g, c):
        for u in range(DMA_UNROLL):
            j = g * DMA_UNROLL + u
            _row_copy(ys_hbm, dest_ref[0, 0, j], ybuf, j, sem).start(priority=u % 2)
        return c

    lax.fori_loop(0, 2 * tm // DMA_UNROLL, issue, 0)
    pltpu.make_async_copy(ys_hbm.at[pl.ds(0, 2 * tm)], ybuf, sem.at[0]).wait()
    w = w_ref[...]
    y = w[:, 0:1] * ybuf[0:tm, :] + w[:, 1:2] * ybuf[tm:2 * tm, :]
    o_ref[...] = _layernorm(ALPHA * x_ref[...] + y, g_ref[...], b_ref[...])


def _combine(plan, x, rw, ys, g, b):
    t = x.shape[0]
    tm = COMBINE_TM
    dest = _tile_dest(plan, t, tm)
    full = lambda a: pl.BlockSpec(a.shape, lambda i: (0, 0))
    return pl.pallas_call(
        _combine_kernel,
        grid=(t // tm,),
        in_specs=[pl.BlockSpec((1, 1, 2 * tm), lambda i: (i, 0, 0), memory_space=pltpu.SMEM),
                  pl.BlockSpec((tm, D_MODEL), lambda i: (i, 0)),
                  pl.BlockSpec((tm, LANES), lambda i: (i, 0)),
                  pl.BlockSpec(memory_space=pl.ANY), full(g), full(b)],
        out_specs=pl.BlockSpec((tm, D_MODEL), lambda i: (i, 0)),
        out_shape=jax.ShapeDtypeStruct((t, D_MODEL), F32),
        scratch_shapes=[pltpu.VMEM((2 * tm, D_MODEL), F32), pltpu.SemaphoreType.DMA((1,))],
        compiler_params=pltpu.CompilerParams(dimension_semantics=("arbitrary",)),
        name="moe_combine_ln",
    )(dest, x, rw, ys, g, b)


def kernel(x_prompt, x_sample, w_in, mla_q_norm, mla_w_qb, mla_kv_norm, mla_w_kvb,
           gqa_q_norm, gqa_k_norm, diff_lambda, diff_out_norm, w_out, ln1_g, ln1_b,
           ffn_w_gate, ffn_w_up, ffn_w_down, moe_router, moe_w_gate, moe_w_up,
           moe_w_down, ln2_g, ln2_b):
    tables = _rope_tables()
    layers = [_layer_weights(l, w_in, mla_q_norm, mla_w_qb, mla_kv_norm, mla_w_kvb,
                             gqa_q_norm, gqa_k_norm, w_out) for l in range(DEPTH)]
    ffn_w = (ffn_w_gate.astype(BF16), ffn_w_up.astype(BF16), ffn_w_down.astype(BF16))
    moe_w = (moe_w_gate.astype(BF16), moe_w_up.astype(BF16), moe_w_down.astype(BF16))
    routers = []
    for m in range(moe_router.shape[0]):
        wr = jnp.pad(moe_router[m], ((0, 0), (0, LANES - N_EXPERTS)))
        wr_hi = wr.astype(BF16)
        routers.append((wr_hi, (wr - wr_hi.astype(F32)).astype(BF16)))

    def run(x):
        b, s, _ = x.shape
        t = b * s
        xf = x.reshape(t, D_MODEL)
        for l in range(DEPTH):
            lw = layers[l]
            mq, mk, mvt, gq, gk, gvt, dq, dk, dvt = _prep(xf, tables, lw, s)
            r3 = lambda a: a.reshape(b, s, a.shape[-1])
            o_m = _attention(r3(mq), r3(mk), mvt, mode='mla')
            o_g = _attention(r3(gq), r3(gk), gvt, mode='gqa')
            lam_init = 0.8 - 0.6 * math.exp(-0.3 * l)
            o_d = _attention(r3(dq), r3(dk), dvt, mode='diff',
                             lam_params=diff_lambda[l].astype(F32),
                             gain=jnp.tile(diff_out_norm[l], 2).reshape(1, LANES),
                             lam_init=lam_init)
            mix = (o_m.reshape(t, -1), o_g.reshape(t, -1), o_d.reshape(t, -1))
            g1, b1 = ln1_g[l].reshape(1, -1), ln1_b[l].reshape(1, -1)
            g2, b2 = ln2_g[l].reshape(1, -1), ln2_b[l].reshape(1, -1)
            m = l // 2
            if l % 2 == 0:
                x1 = _outproj(xf, *mix, lw, g1, b1)
                xf = _ffn(x1, ffn_w[0][m], ffn_w[1][m], ffn_w[2][m], g2, b2)
            else:
                x1, ri, rw = _outproj(xf, *mix, lw, g1, b1, router=routers[m])
                plan = _route_plan(ri, FFN_TM)
                xs = _dispatch(plan, x1, FFN_TM)
                ys = _grouped_ffn(plan, xs, moe_w[0][m], moe_w[1][m], moe_w[2][m], FFN_TM)
                xf = _combine(plan, x1, rw, ys, g2, b2)
        return xf.reshape(b, s, D_MODEL)

    return (run(x_prompt), run(x_sample))
```
